```python
import jax, jax.numpy as jnp
from jax import lax
import numpy as np

D_MODEL = 4096
BATCH = 16
SEQ = 256
DEPTH = 4
DEC_BATCH = 2
DEC_SEQ = 4096
PAST_LEN = 512

GRID_W = 64
HEAD_DIM = 128
HEADS_A = 16
KV_HEADS_A = 4
HEADS_B = 16
KV_HEADS_B = 4
Q_BLOCK = 128
WINDOW = 128
BAND = Q_BLOCK + 2 * WINDOW
MLA_HEADS = 32
Q_LORA = 1024
KV_LORA = 512
NOPE_DIM = 128
ROPE_DIM = 64
V_DIM = 128
D_FF = 8192
N_EXPERTS = 8
TOP_K = 2
EXPERT_FF = 2048
ROPE_THETA = 10000.0
EPS = 1e-6
NEG_INF = -1e30
N_EVEN = (DEPTH + 1) // 2
N_ODD = DEPTH // 2
ALPHA = (2.0 * DEPTH) ** 0.25
INIT_BETA = (8.0 * DEPTH) ** -0.25
AB_SPLITS = (HEADS_A * HEAD_DIM, KV_HEADS_A * HEAD_DIM, KV_HEADS_A * HEAD_DIM,
             HEADS_B * HEAD_DIM, KV_HEADS_B * HEAD_DIM, KV_HEADS_B * HEAD_DIM)
QKV_AB_DIM = sum(AB_SPLITS)

kernel_name = 'hybrid_flow_trunk_step'


def _rms(x, g):
    xf = x.astype(jnp.float32)
    y = xf * lax.rsqrt(jnp.mean(xf * xf, axis=-1, keepdims=True) + EPS) * g.astype(jnp.float32)
    return y.astype(x.dtype)


def _layernorm(x, g, b):
    xf = x.astype(jnp.float32)
    mu = jnp.mean(xf, axis=-1, keepdims=True)
    var = jnp.mean(jnp.square(xf - mu), axis=-1, keepdims=True)
    y = (xf - mu) * lax.rsqrt(var + EPS) * g.astype(jnp.float32) + b.astype(jnp.float32)
    return y.astype(x.dtype)


def _split_cols(x, sizes):
    out, start = [], 0
    for s in sizes:
        out.append(x[..., start:start + s])
        start += s
    return out


def _axial_rope_tables(n_tokens, dim):
    n_rows = n_tokens // GRID_W
    row = jnp.repeat(jnp.arange(n_rows, dtype=jnp.float32), GRID_W)
    col = (jnp.arange(n_tokens) % GRID_W).astype(jnp.float32)
    d_axis = dim // 2
    inv = ROPE_THETA ** (-jnp.arange(0, d_axis, 2, dtype=jnp.float32) / d_axis)
    ang = jnp.stack([row[:, None] * inv, col[:, None] * inv], axis=1)
    return jnp.cos(ang), jnp.sin(ang)


def _apply_rope(x, rope):
    cos, sin = rope
    b, t, h, dim = x.shape
    xr = x.reshape(b, t, h, 2, 2, dim // 4).astype(jnp.float32)
    x1, x2 = xr[..., 0, :], xr[..., 1, :]
    c = cos[None, :, None]
    s = sin[None, :, None]
    out = jnp.stack([x1 * c - x2 * s, x1 * s + x2 * c], axis=-2)
    return out.reshape(b, t, h, dim).astype(x.dtype)


def _group(q, n_kv):
    b, t, h, d = q.shape
    return q.reshape(b, t, n_kv, h // n_kv, d)


def _blocked_attention(q, k, v, scale, sink=None):
    b, t, hkv, g, dq = q.shape
    dv = v.shape[-1]
    n_keys = k.shape[1]
    nb = t // Q_BLOCK
    qb = q.reshape(b, nb, Q_BLOCK, hkv, g, dq).transpose(1, 0, 2, 3, 4, 5)

    def one_block(qi):
        s = jnp.einsum('bqhgd,bshd->bhgqs', qi, k, preferred_element_type=jnp.float32) * scale
        if sink is not None:
            sk = jnp.broadcast_to(sink.astype(jnp.float32)[None, :, :, None, None], s.shape[:-1] + (1,))
            s = jnp.concatenate([s, sk], axis=-1)
        p = jax.nn.softmax(s, axis=-1)[..., :n_keys]
        return jnp.einsum('bhgqs,bshd->bqhgd', p.astype(v.dtype), v)

    o = lax.map(one_block, qb)
    return o.transpose(1, 0, 2, 3, 4, 5).reshape(b, t, hkv, g, dv)


def _window_sink_attention(q, k, v, k_ctx, v_ctx, sink, scale):
    b, t, hkv, g, d = q.shape
    nb = t // Q_BLOCK
    n_ctx = k_ctx.shape[1]
    pad = ((0, 0), (WINDOW, WINDOW), (0, 0), (0, 0))
    idx = jnp.arange(nb)[:, None] * Q_BLOCK + jnp.arange(BAND)[None, :]
    k_band = jnp.pad(k, pad)[:, idx]
    v_band = jnp.pad(v, pad)[:, idx]
    q_pos = jnp.arange(nb)[:, None] * Q_BLOCK + jnp.arange(Q_BLOCK)[None, :]
    k_pos = idx - WINDOW
    valid = ((k_pos[:, None, :] >= 0) & (k_pos[:, None, :] < t)
             & (jnp.abs(q_pos[:, :, None] - k_pos[:, None, :]) <= WINDOW))
    qb = q.reshape(b, nb, Q_BLOCK, hkv, g, d)
    s_loc = jnp.einsum('bnqhgd,bnshd->bnhgqs', qb, k_band, preferred_element_type=jnp.float32) * scale
    s_loc = jnp.where(valid[None, :, None, None], s_loc, NEG_INF)
    s_ctx = jnp.einsum('bnqhgd,bshd->bnhgqs', qb, k_ctx, preferred_element_type=jnp.float32) * scale
    s_sink = jnp.broadcast_to(sink.astype(jnp.float32)[None, None, :, :, None, None], s_loc.shape[:-1] + (1,))
    p = jax.nn.softmax(jnp.concatenate([s_loc, s_ctx, s_sink], axis=-1), axis=-1)
    p_loc = p[..., :BAND].astype(v.dtype)
    p_ctx = p[..., BAND:BAND + n_ctx].astype(v.dtype)
    o = (jnp.einsum('bnhgqs,bnshd->bnqhgd', p_loc, v_band)
         + jnp.einsum('bnhgqs,bshd->bnqhgd', p_ctx, v_ctx))
    return o.reshape(b, t, hkv, g, d)


def _ab_mixer(h, w_qkv, qn_g, kn_g, sink, w_o, rope, ctx):
    b, t, _ = h.shape
    qa, ka, va, qb, kb, vb = _split_cols(h @ w_qkv, AB_SPLITS)
    qa = _rms(qa.reshape(b, t, HEADS_A, HEAD_DIM), qn_g)
    ka = _rms(ka.reshape(b, t, KV_HEADS_A, HEAD_DIM), kn_g)
    va = va.reshape(b, t, KV_HEADS_A, HEAD_DIM)
    qb = qb.reshape(b, t, HEADS_B, HEAD_DIM)
    kb = kb.reshape(b, t, KV_HEADS_B, HEAD_DIM)
    vb = vb.reshape(b, t, KV_HEADS_B, HEAD_DIM)
    sink_g = sink.reshape(KV_HEADS_B, HEADS_B // KV_HEADS_B)
    scale = HEAD_DIM ** -0.5
    if ctx is None:
        oa = _blocked_attention(_group(qa, KV_HEADS_A), ka, va, scale)
        ob = _blocked_attention(_group(qb, KV_HEADS_B), kb, vb, scale, sink_g)
        state = (ka, va, kb, vb)
    else:
        ka_c, va_c, kb_c, vb_c = ctx
        qa, ka = _apply_rope(qa, rope), _apply_rope(ka, rope)
        qb, kb = _apply_rope(qb, rope), _apply_rope(kb, rope)
        oa = _blocked_attention(_group(qa, KV_HEADS_A), jnp.concatenate([ka, ka_c], axis=1),
                                jnp.concatenate([va, va_c], axis=1), scale)
        ob = _window_sink_attention(_group(qb, KV_HEADS_B), kb, vb, kb_c, vb_c, sink_g, scale)
        state = None
    o = jnp.concatenate([oa.reshape(b, t, HEADS_A * HEAD_DIM), ob.reshape(b, t, HEADS_B * HEAD_DIM)], axis=-1)
    return o @ w_o, state


def _mla_mixer(h, w_dq, qn_g, w_uq, w_dkv, kvn_g, w_ukv, w_o, rope, ctx):
    b, t, _ = h.shape
    q = (_rms(h @ w_dq, qn_g) @ w_uq).reshape(b, t, MLA_HEADS, NOPE_DIM + ROPE_DIM)
    q_nope, q_rope = q[..., :NOPE_DIM], q[..., NOPE_DIM:]
    kv_a = h @ w_dkv
    c_kv = _rms(kv_a[..., :KV_LORA], kvn_g)
    k_rope = kv_a[..., KV_LORA:]
    if ctx is None:
        state = (c_kv, k_rope)
        c_all, kr_all = c_kv, k_rope
    else:
        q_rope = _apply_rope(q_rope, rope)
        k_rope = _apply_rope(k_rope[:, :, None, :], rope)[:, :, 0, :]
        c_all = jnp.concatenate([c_kv, ctx[0]], axis=1)
        kr_all = jnp.concatenate([k_rope, ctx[1]], axis=1)
        state = None
    s_len = c_all.shape[1]
    kv = (c_all @ w_ukv).reshape(b, s_len, MLA_HEADS, NOPE_DIM + V_DIM)
    k = jnp.concatenate([kv[..., :NOPE_DIM],
                         jnp.broadcast_to(kr_all[:, :, None, :], (b, s_len, MLA_HEADS, ROPE_DIM))], axis=-1)
    v = kv[..., NOPE_DIM:]
    qf = jnp.concatenate([q_nope, q_rope], axis=-1)[:, :, :, None, :]
    o = _blocked_attention(qf, k, v, (NOPE_DIM + ROPE_DIM) ** -0.5)
    return o.reshape(b, t, MLA_HEADS * V_DIM) @ w_o, state


def _swiglu(h, w_gate, w_up, w_down):
    return (jax.nn.silu(h @ w_gate) * (h @ w_up)) @ w_down


def _moe_swiglu(h, w_router, we_gate, we_up, we_down):
    b, t, d = h.shape
    xt = h.reshape(-1, d)
    logits = (xt @ w_router).astype(jnp.float32)
    top_v, top_i = lax.top_k(logits, TOP_K)
    gates = jax.nn.softmax(top_v, axis=-1)
    combine = jnp.sum(jax.nn.one_hot(top_i, N_EXPERTS, dtype=jnp.float32) * gates[..., None], axis=1)
    g = jnp.einsum('nd,edf->nef', xt, we_gate)
    u = jnp.einsum('nd,edf->nef', xt, we_up)
    act = jax.nn.silu(g) * u * combine.astype(xt.dtype)[:, :, None]
    out = jnp.einsum('nef,efd->nd', act, we_down)
    return out.reshape(b, t, d)


def _trunk(x, cond, cache, rope_ab, rope_mla, weights):
    (w_mod, b_mod, ln1_g, ln1_b, ln2_g, ln2_b, w_qkv_ab, qn_a_g, kn_a_g, sink_b, w_o_ab,
     w_ff_gate, w_ff_up, w_ff_down, w_dq, qn_c_g, w_uq, w_dkv, kvn_c_g, w_ukv, w_o_c,
     w_router, we_gate, we_up, we_down) = weights
    states_ab, states_c = [], []
    for i in range(DEPTH):
        j = i // 2
        mod = (jax.nn.silu(cond) @ w_mod[i] + b_mod[i])[:, None, :]
        sh1, sc1, g1, sh2, sc2, g2 = jnp.split(mod, 6, axis=-1)
        h = x * (1.0 + sc1) + sh1
        if i % 2 == 0:
            ctx = None if cache is None else tuple(cc[:, j] for cc in cache[:4])
            out, st = _ab_mixer(h, w_qkv_ab[j], qn_a_g[j], kn_a_g[j], sink_b[j], w_o_ab[j], rope_ab, ctx)
            states_ab.append(st)
        else:
            ctx = None if cache is None else tuple(cc[:, j] for cc in cache[4:])
            out, st = _mla_mixer(h, w_dq[j], qn_c_g[j], w_uq[j], w_dkv[j], kvn_c_g[j], w_ukv[j], w_o_c[j],
                                 rope_mla, ctx)
            states_c.append(st)
        x = _layernorm(ALPHA * x + g1 * out, ln1_g[i], ln1_b[i])
        h = x * (1.0 + sc2) + sh2
        if i % 2 == 0:
            f = _swiglu(h, w_ff_gate[j], w_ff_up[j], w_ff_down[j])
        else:
            f = _moe_swiglu(h, w_router[j], we_gate[j], we_up[j], we_down[j])
        x = _layernorm(ALPHA * x + g2 * f, ln2_g[i], ln2_b[i])
    return x, states_ab, states_c


def setup_inputs(seed: int = 0) -> dict:
    key = jax.random.key(seed)
    ks = iter(jax.random.split(key, 48))
    f32 = jnp.float32

    def nrm(shape, scale=1.0):
        return jax.random.normal(next(ks), shape, f32) * scale

    d = D_MODEL
    inp = {}
    inp['x_prompt'] = nrm((BATCH, SEQ, d))
    inp['x_sample'] = nrm((DEC_BATCH, DEC_SEQ, d))
    inp['cache_k_a'] = nrm((DEC_BATCH, N_EVEN, PAST_LEN, KV_HEADS_A, HEAD_DIM))
    inp['cache_v_a'] = nrm((DEC_BATCH, N_EVEN, PAST_LEN, KV_HEADS_A, HEAD_DIM))
    inp['cache_k_b'] = nrm((DEC_BATCH, N_EVEN, PAST_LEN, KV_HEADS_B, HEAD_DIM))
    inp['cache_v_b'] = nrm((DEC_BATCH, N_EVEN, PAST_LEN, KV_HEADS_B, HEAD_DIM))
    inp['cache_ckv'] = nrm((DEC_BATCH, N_ODD, PAST_LEN, KV_LORA))
    inp['cache_krope'] = nrm((DEC_BATCH, N_ODD, PAST_LEN, ROPE_DIM))
    inp['c'] = nrm((DEC_BATCH, d))
    inp['c_ctx'] = nrm((d,))
    inp['w_mod'] = nrm((DEPTH, d, 6 * d), 0.5 * d ** -0.5)
    inp['b_mod'] = nrm((DEPTH, 6 * d), 0.02)
    inp['ln1_g'] = 1.0 + nrm((DEPTH, d), 0.02)
    inp['ln1_b'] = nrm((DEPTH, d), 0.02)
    inp['ln2_g'] = 1.0 + nrm((DEPTH, d), 0.02)
    inp['ln2_b'] = nrm((DEPTH, d), 0.02)
    inp['w_qkv_ab'] = nrm((N_EVEN, d, QKV_AB_DIM), d ** -0.5)
    inp['qn_a_g'] = 1.0 + nrm((N_EVEN, HEAD_DIM), 0.02)
    inp['kn_a_g'] = 1.0 + nrm((N_EVEN, HEAD_DIM), 0.02)
    inp['sink_b'] = nrm((N_EVEN, HEADS_B), 0.5)
    inp['w_o_ab'] = nrm((N_EVEN, (HEADS_A + HEADS_B) * HEAD_DIM, d), INIT_BETA * ((HEADS_A + HEADS_B) * HEAD_DIM) ** -0.5)
    inp['w_ff_gate'] = nrm((N_EVEN, d, D_FF), d ** -0.5)
    inp['w_ff_up'] = nrm((N_EVEN, d, D_FF), d ** -0.5)
    inp['w_ff_down'] = nrm((N_EVEN, D_FF, d), INIT_BETA * D_FF ** -0.5)
    inp['w_dq'] = nrm((N_ODD, d, Q_LORA), d ** -0.5)
    inp['qn_c_g'] = 1.0 + nrm((N_ODD, Q_LORA), 0.02)
    inp['w_uq'] = nrm((N_ODD, Q_LORA, MLA_HEADS * (NOPE_DIM + ROPE_DIM)), Q_LORA ** -0.5)
    inp['w_dkv'] = nrm((N_ODD, d, KV_LORA + ROPE_DIM), d ** -0.5)
    inp['kvn_c_g'] = 1.0 + nrm((N_ODD, KV_LORA), 0.02)
    inp['w_ukv'] = nrm((N_ODD, KV_LORA, MLA_HEADS * (NOPE_DIM + V_DIM)), KV_LORA ** -0.5)
    inp['w_o_c'] = nrm((N_ODD, MLA_HEADS * V_DIM, d), INIT_BETA * (MLA_HEADS * V_DIM) ** -0.5)
    inp['w_router'] = nrm((N_ODD, d, N_EXPERTS), d ** -0.5)
    inp['we_gate'] = nrm((N_ODD, N_EXPERTS, d, EXPERT_FF), d ** -0.5)
    inp['we_up'] = nrm((N_ODD, N_EXPERTS, d, EXPERT_FF), d ** -0.5)
    inp['we_down'] = nrm((N_ODD, N_EXPERTS, EXPERT_FF, d), INIT_BETA * EXPERT_FF ** -0.5)
    return inp


def reference(x_prompt, x_sample, cache_k_a, cache_v_a, cache_k_b, cache_v_b, cache_ckv, cache_krope,
              c, c_ctx, w_mod, b_mod, ln1_g, ln1_b, ln2_g, ln2_b, w_qkv_ab, qn_a_g, kn_a_g, sink_b, w_o_ab,
              w_ff_gate, w_ff_up, w_ff_down, w_dq, qn_c_g, w_uq, w_dkv, kvn_c_g, w_ukv, w_o_c,
              w_router, we_gate, we_up, we_down):
    weights = (w_mod, b_mod, ln1_g, ln1_b, ln2_g, ln2_b, w_qkv_ab, qn_a_g, kn_a_g, sink_b, w_o_ab,
               w_ff_gate, w_ff_up, w_ff_down, w_dq, qn_c_g, w_uq, w_dkv, kvn_c_g, w_ukv, w_o_c,
               w_router, we_gate, we_up, we_down)
    y_prompt, st_ab, st_c = _trunk(x_prompt, c_ctx[None, :], None, None, None, weights)
    new_k_a = jnp.stack([s[0] for s in st_ab], axis=1)
    new_v_a = jnp.stack([s[1] for s in st_ab], axis=1)
    new_k_b = jnp.stack([s[2] for s in st_ab], axis=1)
    new_v_b = jnp.stack([s[3] for s in st_ab], axis=1)
    new_ckv = jnp.stack([s[0] for s in st_c], axis=1)
    new_krope = jnp.stack([s[1] for s in st_c], axis=1)
    n_lat = x_sample.shape[1]
    rope_ab = _axial_rope_tables(n_lat, HEAD_DIM)
    rope_mla = _axial_rope_tables(n_lat, ROPE_DIM)
    cache = (cache_k_a, cache_v_a, cache_k_b, cache_v_b, cache_ckv, cache_krope)
    y_sample, _, _ = _trunk(x_sample, c, cache, rope_ab, rope_mla, weights)
    return (y_prompt, y_sample, new_k_a, new_v_a, new_k_b, new_v_b, new_ckv, new_krope)
```

```python
import functools

import jax
import jax.numpy as jnp
from jax import lax
from jax.experimental import pallas as pl
from jax.experimental.pallas import tpu as pltpu

F32 = jnp.float32
BF16 = jnp.bfloat16

GRID_W = 64
WINDOW = 128
ROPE_THETA = 10000.0
EPS = 1e-6
NEG_INF = -1e30
TOP_K = 2
LANES = 128
VMEM_LIMIT_BYTES = 56 * 2**20


def _params(*sem):
    return pltpu.CompilerParams(dimension_semantics=sem, vmem_limit_bytes=VMEM_LIMIT_BYTES)


def _tile(n, pref):
    t = min(n, pref)
    while n % t:
        t //= 2
    return t


def _silu(x):
    return x / (1.0 + jnp.exp(-x))


def _mod_body(c_ref, w_ref, b_ref, o_ref):
    s = _silu(c_ref[...]).astype(BF16)
    o_ref[...] = jnp.dot(s, w_ref[...].astype(BF16), preferred_element_type=F32) + b_ref[...]


def _mod_call(cond, w_mod, b_mod):
    depth, d, n = w_mod.shape
    r = cond.shape[0]
    tn = _tile(n, 512)
    return pl.pallas_call(
        _mod_body,
        out_shape=jax.ShapeDtypeStruct((depth, r, n), F32),
        grid=(depth, n // tn),
        in_specs=[
            pl.BlockSpec((r, d), lambda l, j: (0, 0)),
            pl.BlockSpec((None, d, tn), lambda l, j: (l, 0, j)),
            pl.BlockSpec((None, 1, tn), lambda l, j: (l, 0, j)),
        ],
        out_specs=pl.BlockSpec((None, r, tn), lambda l, j: (l, 0, j)),
        compiler_params=_params("parallel", "parallel"),
        name="mod",
    )(cond, w_mod, b_mod.reshape(depth, 1, n))


def _cond_of_tile(i, tm, n_prompt, t_dec):
    np_tiles = n_prompt // tm
    return jnp.where(i < np_tiles, 0, 1 + (i - np_tiles) // (t_dec // tm))


def _mod_spec(layer, tm, n_prompt, t_dec, d):
    return pl.BlockSpec((None, None, 6, d), lambda i: (layer, _cond_of_tile(i, tm, n_prompt, t_dec), 0, 0))


def _modulate_body(x_ref, mod_ref, h_ref):
    sh = mod_ref[0:1, :]
    sc = mod_ref[1:2, :]
    h_ref[...] = (x_ref[...] * (1.0 + sc) + sh).astype(h_ref.dtype)


def _modulate_call(x, mod4, layer, n_prompt, t_dec):
    nt, d = x.shape
    tm = _tile(min(n_prompt, t_dec), 512)
    return pl.pallas_call(
        _modulate_body,
        out_shape=jax.ShapeDtypeStruct((nt, d), BF16),
        grid=(nt // tm,),
        in_specs=[pl.BlockSpec((tm, d), lambda i: (i, 0)), _mod_spec(layer, tm, n_prompt, t_dec, d)],
        out_specs=pl.BlockSpec((tm, d), lambda i: (i, 0)),
        compiler_params=_params("parallel"),
        name="modulate",
    )(x, mod4)


def _ln_body(*refs, alpha, gate_row, next_row, has_next, has_router, n_experts):
    it = iter(refs)
    x_ref, s_ref, modc_ref, g_ref, b_ref = next(it), next(it), next(it), next(it), next(it)
    modn_ref = next(it) if has_next else None
    wr_ref = next(it) if has_router else None
    xo_ref = next(it)
    h_ref = next(it) if has_next else None
    comb_ref = next(it) if has_router else None

    gate = modc_ref[gate_row:gate_row + 1, :]
    y = alpha * x_ref[...] + gate * s_ref[...]
    mu = jnp.mean(y, axis=-1, keepdims=True)
    yc = y - mu
    var = jnp.mean(yc * yc, axis=-1, keepdims=True)
    xn = yc * lax.rsqrt(var + EPS) * g_ref[...] + b_ref[...]
    xo_ref[...] = xn
    if has_next:
        sh = modn_ref[next_row:next_row + 1, :]
        sc = modn_ref[next_row + 1:next_row + 2, :]
        h = xn * (1.0 + sc) + sh
        h_ref[...] = h.astype(h_ref.dtype)
    if has_router:
        logits = jnp.dot(h, wr_ref[...], preferred_element_type=F32, precision=lax.Precision.HIGHEST)
        lane = lax.broadcasted_iota(jnp.int32, logits.shape, 1)
        logits = jnp.where(lane < n_experts, logits, -jnp.inf)
        m1 = jnp.max(logits, axis=-1, keepdims=True)
        i1 = jnp.min(jnp.where(logits == m1, lane, LANES), axis=-1, keepdims=True)
        rest = jnp.where(lane == i1, -jnp.inf, logits)
        m2 = jnp.max(rest, axis=-1, keepdims=True)
        i2 = jnp.min(jnp.where(rest == m2, lane, LANES), axis=-1, keepdims=True)
        e2 = jnp.exp(m2 - m1)
        g1 = 1.0 / (1.0 + e2)
        g2 = e2 / (1.0 + e2)
        comb_ref[...] = jnp.where(lane == i1, g1, 0.0) + jnp.where(lane == i2, g2, 0.0)


def _ln_call(x, sub, mod4, ln_g, ln_b, *, layer, gate_row, alpha, n_prompt, t_dec,
             next_layer=None, next_row=None, w_router=None, n_experts=0):
    nt, d = x.shape
    tm = _tile(min(n_prompt, t_dec), 256)
    has_next = next_layer is not None
    has_router = w_router is not None
    row = pl.BlockSpec((tm, d), lambda i: (i, 0))
    vec = pl.BlockSpec((None, 1, d), lambda i: (layer, 0, 0))
    in_specs = [row, row, _mod_spec(layer, tm, n_prompt, t_dec, d), vec, vec]
    args = [x, sub, mod4, ln_g.reshape(-1, 1, d), ln_b.reshape(-1, 1, d)]
    out_shape = [jax.ShapeDtypeStruct((nt, d), F32)]
    out_specs = [row]
    if has_next:
        in_specs.append(_mod_spec(next_layer, tm, n_prompt, t_dec, d))
        args.append(mod4)
        out_shape.append(jax.ShapeDtypeStruct((nt, d), BF16))
        out_specs.append(row)
    if has_router:
        in_specs.append(pl.BlockSpec((d, LANES), lambda i: (0, 0)))
        args.append(w_router)
        out_shape.append(jax.ShapeDtypeStruct((nt, LANES), F32))
        out_specs.append(pl.BlockSpec((tm, LANES), lambda i: (i, 0)))
    body = functools.partial(_ln_body, alpha=alpha, gate_row=gate_row, next_row=next_row,
                             has_next=has_next, has_router=has_router, n_experts=n_experts)
    return pl.pallas_call(
        body, out_shape=out_shape, grid=(nt // tm,), in_specs=in_specs, out_specs=out_specs,
        compiler_params=_params("parallel"), name="ln_mod",
    )(*args)


def _rms_rows(x, g):
    return x * lax.rsqrt(jnp.mean(x * x, axis=-1, keepdims=True) + EPS) * g


def _rope_lanes(x, c, s1, s2, seg):
    return x * c + pltpu.roll(x, LANES - seg, 1) * s1 + pltpu.roll(x, seg, 1) * s2


def _epi_none(acc, o_refs):
    o_refs[0][...] = acc.astype(o_refs[0].dtype)


def _epi_rms(acc, g_ref, o_refs):
    o_refs[0][...] = _rms_rows(acc, g_ref[...]).astype(o_refs[0].dtype)


def _epi_mla_q(acc, c_ref, s1_ref, s2_ref, o_refs, *, seg):
    n_heads = acc.shape[1] // (2 * LANES)
    c, s1, s2 = c_ref[...], s1_ref[...], s2_ref[...]
    for h in range(n_heads):
        lo = h * 2 * LANES
        o_refs[0][:, lo:lo + LANES] = acc[:, lo:lo + LANES].astype(o_refs[0].dtype)
        o_refs[0][:, lo + LANES:lo + 2 * LANES] = _rope_lanes(
            acc[:, lo + LANES:lo + 2 * LANES], c, s1, s2, seg).astype(o_refs[0].dtype)


def _epi_mla_kv(acc, g_ref, c_ref, s1_ref, s2_ref, o_refs, *, kv_lora, seg):
    ckv = _rms_rows(acc[:, :kv_lora], g_ref[...])
    kr = acc[:, kv_lora:kv_lora + LANES]
    o_refs[0][:, :kv_lora] = ckv
    o_refs[0][:, kv_lora:kv_lora + LANES] = kr
    o_refs[1][...] = ckv.astype(o_refs[1].dtype)
    o_refs[2][...] = _rope_lanes(kr, c_ref[...], s1_ref[...], s2_ref[...], seg).astype(o_refs[2].dtype)


def _mm_body(*refs, nk, n_extra, n_out, epi):
    x_ref, w_ref = refs[0], refs[1]
    extra = refs[2:2 + n_extra]
    o_refs = refs[2 + n_extra:2 + n_extra + n_out]
    part = jnp.dot(x_ref[...], w_ref[...].astype(BF16), preferred_element_type=F32)
    if nk == 1:
        epi(part, *extra, o_refs)
    else:
        acc_ref = refs[2 + n_extra + n_out]
        k = pl.program_id(2)

        @pl.when(k == 0)
        def _():
            acc_ref[...] = part

        @pl.when(k > 0)
        def _():
            acc_ref[...] += part

        @pl.when(k == nk - 1)
        def _():
            epi(acc_ref[...], *extra, o_refs)


def _matmul(x, w, *, w_map, n, tm, tn, tk=None, out=None, epi=_epi_none, extra=(), extra_specs=(), name="mm"):
    m, kdim = x.shape
    tk = tk or kdim
    nk = kdim // tk
    if out is None:
        out = [(tn, F32, n)]
    w_block = (None,) * (w.ndim - 2) + (tk, tn)
    out_shape = [jax.ShapeDtypeStruct((m, width), dt) for (_, dt, width) in out]
    out_specs = [pl.BlockSpec((tm, bt), lambda i, j, k: (i, j)) for (bt, _, _) in out]
    body = functools.partial(_mm_body, nk=nk, n_extra=len(extra), n_out=len(out), epi=epi)
    res = pl.pallas_call(
        body,
        out_shape=out_shape,
        grid=(m // tm, n // tn, nk),
        in_specs=[pl.BlockSpec((tm, tk), lambda i, j, k: (i, k)),
                  pl.BlockSpec(w_block, lambda i, j, k: w_map(j, k))] + list(extra_specs),
        out_specs=out_specs,
        scratch_shapes=[pltpu.VMEM((tm, tn), F32)] if nk > 1 else [],
        compiler_params=_params("parallel", "parallel", "arbitrary"),
        name=name,
    )(x, w, *extra)
    return res


def _swiglu_body(*refs, has_scale, tiles_per_expert):
    if has_scale:
        x_ref, wg_ref, wu_ref, comb_ref, o_ref = refs
    else:
        x_ref, wg_ref, wu_ref, o_ref = refs
    x = x_ref[...]
    g = jnp.dot(x, wg_ref[...].astype(BF16), preferred_element_type=F32)
    u = jnp.dot(x, wu_ref[...].astype(BF16), preferred_element_type=F32)
    act = _silu(g) * u
    if has_scale:
        e = pl.program_id(1) // tiles_per_expert
        comb = comb_ref[...]
        lane = lax.broadcasted_iota(jnp.int32, comb.shape, 1)
        act = act * jnp.sum(jnp.where(lane == e, comb, 0.0), axis=-1, keepdims=True)
    o_ref[...] = act.astype(o_ref.dtype)


def _swiglu_call(x, wg, wu, *, w_map, n, tm, tn, comb=None, tiles_per_expert=1):
    m, kdim = x.shape
    w_block = (None,) * (wg.ndim - 2) + (kdim, tn)
    w_spec = pl.BlockSpec(w_block, lambda i, j: w_map(j))
    in_specs = [pl.BlockSpec((tm, kdim), lambda i, j: (i, 0)), w_spec, w_spec]
    args = [x, wg, wu]
    if comb is not None:
        in_specs.append(pl.BlockSpec((tm, LANES), lambda i, j: (i, 0)))
        args.append(comb)
    body = functools.partial(_swiglu_body, has_scale=comb is not None, tiles_per_expert=tiles_per_expert)
    return pl.pallas_call(
        body,
        out_shape=jax.ShapeDtypeStruct((m, n), BF16),
        grid=(m // tm, n // tn),
        in_specs=in_specs,
        out_specs=pl.BlockSpec((tm, tn), lambda i, j: (i, j)),
        compiler_params=_params("parallel", "parallel"),
        name="swiglu",
    )(*args)


def _qkv_post_body(qkv_ref, qn_ref, kn_ref, c_ref, s1_ref, s2_ref,
                   qa_ref, ka_ref, va_ref, qb_ref, kb_ref, vb_ref, st_ref, *, ha, ka, hb, kb, seg):
    c, s1, s2 = c_ref[...], s1_ref[...], s2_ref[...]
    qn, kn = qn_ref[...], kn_ref[...]
    col = 0
    st = 0

    def head(idx):
        return qkv_ref[:, idx * LANES:(idx + 1) * LANES]

    for h in range(ha):
        q = _rms_rows(head(col + h), qn)
        qa_ref[:, h * LANES:(h + 1) * LANES] = _rope_lanes(q, c, s1, s2, seg).astype(qa_ref.dtype)
    col += ha
    for h in range(ka):
        k = _rms_rows(head(col + h), kn)
        st_ref[:, (st + h) * LANES:(st + h + 1) * LANES] = k
        ka_ref[:, h * LANES:(h + 1) * LANES] = _rope_lanes(k, c, s1, s2, seg).astype(ka_ref.dtype)
    col += ka
    st += ka
    for h in range(ka):
        v = head(col + h)
        st_ref[:, (st + h) * LANES:(st + h + 1) * LANES] = v
        va_ref[:, h * LANES:(h + 1) * LANES] = v.astype(va_ref.dtype)
    col += ka
    st += ka
    for h in range(hb):
        qb_ref[:, h * LANES:(h + 1) * LANES] = _rope_lanes(head(col + h), c, s1, s2, seg).astype(qb_ref.dtype)
    col += hb
    for h in range(kb):
        k = head(col + h)
        st_ref[:, (st + h) * LANES:(st + h + 1) * LANES] = k
        kb_ref[:, h * LANES:(h + 1) * LANES] = _rope_lanes(k, c, s1, s2, seg).astype(kb_ref.dtype)
    col += kb
    st += kb
    for h in range(kb):
        v = head(col + h)
        st_ref[:, (st + h) * LANES:(st + h + 1) * LANES] = v
        vb_ref[:, h * LANES:(h + 1) * LANES] = v.astype(vb_ref.dtype)


def _qkv_post_call(qkv, qn_g, kn_g, layer, tabs, *, ha, ka, hb, kb, seg):
    nt, width = qkv.shape
    tm = _tile(nt, 256)
    row = lambda w: pl.BlockSpec((tm, w), lambda i: (i, 0))
    gain = pl.BlockSpec((None, 1, LANES), lambda i: (layer, 0, 0))
    widths = [ha * LANES, ka * LANES, ka * LANES, hb * LANES, kb * LANES, kb * LANES]
    st_w = 2 * (ka + kb) * LANES
    body = functools.partial(_qkv_post_body, ha=ha, ka=ka, hb=hb, kb=kb, seg=seg)
    return pl.pallas_call(
        body,
        out_shape=[jax.ShapeDtypeStruct((nt, w), BF16) for w in widths] + [jax.ShapeDtypeStruct((nt, st_w), F32)],
        grid=(nt // tm,),
        in_specs=[row(width), gain, gain, row(LANES), row(LANES), row(LANES)],
        out_specs=[row(w) for w in widths] + [row(st_w)],
        compiler_params=_params("parallel"),
        name="qkv_post",
    )(qkv, qn_g.reshape(-1, 1, LANES), kn_g.reshape(-1, 1, LANES), *tabs)


def _attn_body(*refs, g, dq, dv, tq, tk, t, n_kparts, window, has_ctx, has_sink, scale):
    it = iter(refs)
    q_ref = next(it)
    k_refs = [next(it) for _ in range(n_kparts)]
    v_ref = next(it)
    kc_refs = [next(it) for _ in range(n_kparts)] if has_ctx else None
    vc_ref = next(it) if has_ctx else None
    sink_ref = next(it) if has_sink else None
    o_ref = next(it)

    h = pl.program_id(1)
    qi = pl.program_id(2)
    rows = g * tq
    q = jnp.concatenate([q_ref[:, i * dq:(i + 1) * dq] for i in range(g)], axis=0) if g > 1 else q_ref[...]

    def scores(parts):
        k = parts[0] if len(parts) == 1 else jnp.concatenate(parts, axis=1)
        s = lax.dot_general(q, k.astype(BF16), (((1,), (1,)), ((), ())), preferred_element_type=F32)
        return s * scale

    def update(carry, s, v):
        m, l, acc = carry
        m_new = jnp.maximum(m, jnp.max(s, axis=1, keepdims=True))
        a = jnp.exp(m - m_new)
        p = jnp.exp(s - m_new)
        l = a * l + jnp.sum(p, axis=1, keepdims=True)
        acc = a * acc + jnp.dot(p.astype(BF16), v.astype(BF16), preferred_element_type=F32)
        return m_new, l, acc

    def chunk(c, carry):
        start = pl.multiple_of(c * tk, tk)
        s = scores([r[pl.ds(start, tk), :] for r in k_refs])
        if window is not None:
            qpos = qi * tq + lax.rem(lax.broadcasted_iota(jnp.int32, s.shape, 0), tq)
            kpos = start + lax.broadcasted_iota(jnp.int32, s.shape, 1)
            s = jnp.where(jnp.abs(qpos - kpos) <= window, s, NEG_INF)
        return update(carry, s, v_ref[pl.ds(start, tk), :])

    if window is None:
        lo, hi = 0, t // tk
    else:
        lo = jnp.maximum(qi * tq - window, 0) // tk
        hi = jnp.minimum((qi + 1) * tq + window + tk - 1, t) // tk
    carry = (jnp.full((rows, 1), NEG_INF, F32), jnp.zeros((rows, 1), F32), jnp.zeros((rows, dv), F32))
    carry = lax.fori_loop(lo, hi, chunk, carry)
    if has_ctx:
        carry = update(carry, scores([r[...] for r in kc_refs]), vc_ref[...])
    m, l, acc = carry
    if has_sink:
        sk = jnp.concatenate([jnp.full((tq, 1), sink_ref[0, h * g + i], F32) for i in range(g)], axis=0)
        m_new = jnp.maximum(m, sk)
        a = jnp.exp(m - m_new)
        l = a * l + jnp.exp(sk - m_new)
        acc = a * acc
    out = acc / l
    for i in range(g):
        o_ref[:, i * dv:(i + 1) * dv] = out[i * tq:(i + 1) * tq].astype(o_ref.dtype)


def _attn_call(q, kparts, v, *, nb, t, row0, n_kv, g, dq, dv, tq, tk, k_cols, v_col, scale,
               window=None, ctx=None, sink=None, name="attn"):
    assert row0 % t == 0 and t % tq == 0 and t % tk == 0
    qt = t // tq
    in_specs = [pl.BlockSpec((tq, g * dq), lambda b, h, i: (row0 // tq + b * qt + i, h))]
    args = [q]
    for arr, col in zip(kparts, k_cols):
        in_specs.append(pl.BlockSpec((t, LANES), lambda b, h, i, col=col: (row0 // t + b, col(h))))
        args.append(arr)
    in_specs.append(pl.BlockSpec((t, dv), lambda b, h, i: (row0 // t + b, v_col(h))))
    args.append(v)
    if ctx is not None:
        args += list(ctx[0])
        in_specs += list(ctx[1])
    if sink is not None:
        args.append(sink)
        in_specs.append(pl.BlockSpec(memory_space=pltpu.SMEM))
    body = functools.partial(_attn_body, g=g, dq=dq, dv=dv, tq=tq, tk=tk, t=t, n_kparts=len(kparts),
                             window=window, has_ctx=ctx is not None, has_sink=sink is not None, scale=scale)
    return pl.pallas_call(
        body,
        out_shape=jax.ShapeDtypeStruct((nb * t, n_kv * g * dv), BF16),
        grid=(nb, n_kv, qt),
        in_specs=in_specs,
        out_specs=pl.BlockSpec((tq, g * dv), lambda b, h, i: (b * qt + i, h)),
        compiler_params=_params("parallel", "parallel", "arbitrary"),
        name=name,
    )(*args)


def _rope_tables(n_prompt, n_dec, t_dec, dim):
    seg = dim // 4
    d_axis = dim // 2
    row = jnp.repeat(jnp.arange(t_dec // GRID_W, dtype=F32), GRID_W)
    colp = (jnp.arange(t_dec) % GRID_W).astype(F32)
    inv = ROPE_THETA ** (-jnp.arange(0, d_axis, 2, dtype=F32) / d_axis)
    ar, ac = row[:, None] * inv, colp[:, None] * inv
    z = jnp.zeros_like(ar)
    pad = LANES - dim
    c = jnp.concatenate([jnp.cos(ar), jnp.cos(ar), jnp.cos(ac), jnp.cos(ac), jnp.ones((t_dec, pad), F32)], axis=1)
    s1 = jnp.concatenate([-jnp.sin(ar), z, -jnp.sin(ac), z, jnp.zeros((t_dec, pad), F32)], axis=1)
    s2 = jnp.concatenate([z, jnp.sin(ar), z, jnp.sin(ac), jnp.zeros((t_dec, pad), F32)], axis=1)

    def full(tab, fill):
        return jnp.concatenate([jnp.full((n_prompt, LANES), fill, F32)] + [tab] * n_dec, axis=0)

    return full(c, 1.0), full(s1, 0.0), full(s2, 0.0)


def kernel(x_prompt, x_sample, cache_k_a, cache_v_a, cache_k_b, cache_v_b, cache_ckv, cache_krope, c, c_ctx, w_mod, b_mod, ln1_g, ln1_b, ln2_g, ln2_b, w_qkv_ab, qn_a_g, kn_a_g, sink_b, w_o_ab, w_ff_gate, w_ff_up, w_ff_down, w_dq, qn_c_g, w_uq, w_dkv, kvn_c_g, w_ukv, w_o_c, w_router, we_gate, we_up, we_down):
    nb_p, t_p, d = x_prompt.shape
    nb_d, t_d, _ = x_sample.shape
    n_p, n_d = nb_p * t_p, nb_d * t_d
    nt = n_p + n_d
    depth = w_mod.shape[0]
    alpha = (2.0 * depth) ** 0.25
    past = cache_k_a.shape[2]
    ka, hd = cache_k_a.shape[3], cache_k_a.shape[4]
    kb = cache_k_b.shape[3]
    hb = sink_b.shape[1]
    ha = w_o_ab.shape[1] // hd - hb
    q_lora = w_dq.shape[-1]
    kv_lora = cache_ckv.shape[-1]
    rope_dim = cache_krope.shape[-1]
    n_e, ff_e = we_gate.shape[1], we_gate.shape[3]
    d_ff = w_ff_gate.shape[-1]
    mla_h = (w_uq.shape[-1] - w_ukv.shape[-1] + w_o_c.shape[1]) // rope_dim
    nope = w_uq.shape[-1] // mla_h - rope_dim
    v_dim = w_o_c.shape[1] // mla_h
    assert hd == LANES and nope == LANES and v_dim == LANES and rope_dim <= LANES
    assert nt % past == 0 and n_p % t_p == 0 and n_p % t_d == 0

    x = jnp.concatenate([x_prompt.reshape(n_p, d), x_sample.reshape(n_d, d)], axis=0)
    n_cond = 8
    cond = jnp.concatenate([c_ctx[None, :], c, jnp.zeros((n_cond - 1 - nb_d, d), F32)], axis=0)
    mod4 = _mod_call(cond, w_mod, b_mod).reshape(depth, n_cond, 6, d)

    tabs_ab = _rope_tables(n_p, nb_d, t_d, hd)
    tabs_c = _rope_tables(n_p, nb_d, t_d, rope_dim)

    tm = _tile(min(n_p, t_d), 1024)
    tmap = lambda layer: (lambda j, k: (layer, k, j))

    dq_c = 2 * LANES
    w_uq_p = jnp.pad(w_uq.reshape(w_uq.shape[0], q_lora, mla_h, nope + rope_dim),
                     ((0, 0), (0, 0), (0, 0), (0, dq_c - nope - rope_dim))).reshape(w_uq.shape[0], q_lora, mla_h * dq_c)
    dkv_w = kv_lora + LANES
    w_dkv_p = jnp.pad(w_dkv, ((0, 0), (0, 0), (0, dkv_w - w_dkv.shape[-1])))
    w_router_p = jnp.pad(w_router, ((0, 0), (0, 0), (0, LANES - n_e)))

    h = _modulate_call(x, mod4, 0, n_p, t_d)
    states_ab, states_c = [], []
    for layer in range(depth):
        j = layer // 2
        if layer % 2 == 0:
            n_qkv = w_qkv_ab.shape[-1]
            (qkv,) = _matmul(h, w_qkv_ab, w_map=tmap(j), n=n_qkv, tm=tm, tn=_tile(n_qkv, 512), name="qkv")
            qa, k_a, v_a, qb, k_b, v_b, st = _qkv_post_call(qkv, qn_a_g, kn_a_g, j, tabs_ab,
                                                            ha=ha, ka=ka, hb=hb, kb=kb, seg=hd // 4)
            states_ab.append(st[:n_p])
            scale = hd ** -0.5
            ident = lambda hh: hh
            ck = lambda arr: arr.reshape(nb_d, arr.shape[1], past, -1)
            ctx_spec = pl.BlockSpec((None, None, past, LANES), lambda b, hh, i: (b, j, 0, hh))
            common = dict(k_cols=[ident], v_col=ident, dq=hd, dv=hd, scale=scale)
            oa_p = _attn_call(qa, [k_a], v_a, nb=nb_p, t=t_p, row0=0, n_kv=ka, g=ha // ka,
                              tq=_tile(t_p, 256), tk=_tile(t_p, 512), name="attn_a_p", **common)
            ob_p = _attn_call(qb, [k_b], v_b, nb=nb_p, t=t_p, row0=0, n_kv=kb, g=hb // kb,
                              tq=_tile(t_p, 256), tk=_tile(t_p, 512), sink=sink_b[j:j + 1], name="attn_b_p", **common)
            oa_d = _attn_call(qa, [k_a], v_a, nb=nb_d, t=t_d, row0=n_p, n_kv=ka, g=ha // ka,
                              tq=_tile(t_d, 256), tk=_tile(t_d, 512),
                              ctx=([ck(cache_k_a), ck(cache_v_a)], [ctx_spec, ctx_spec]), name="attn_a_d", **common)
            ob_d = _attn_call(qb, [k_b], v_b, nb=nb_d, t=t_d, row0=n_p, n_kv=kb, g=hb // kb,
                              tq=_tile(t_d, 256), tk=WINDOW, window=WINDOW,
                              ctx=([ck(cache_k_b), ck(cache_v_b)], [ctx_spec, ctx_spec]),
                              sink=sink_b[j:j + 1], name="attn_b_d", **common)
            o = jnp.concatenate([jnp.concatenate([oa_p, ob_p], axis=1), jnp.concatenate([oa_d, ob_d], axis=1)], axis=0)
            (sub,) = _matmul(o, w_o_ab, w_map=tmap(j), n=d, tm=tm, tn=_tile(d, 512), name="wo_ab")
        else:
            (dqn,) = _matmul(h, w_dq, w_map=tmap(j), n=q_lora, tm=tm, tn=q_lora, tk=_tile(d, 2048),
                             out=[(q_lora, BF16, q_lora)], epi=_epi_rms,
                             extra=[qn_c_g.reshape(-1, 1, q_lora)],
                             extra_specs=[pl.BlockSpec((None, 1, q_lora), lambda i, jj, k: (j, 0, 0))], name="dq")
            tab_specs = [pl.BlockSpec((tm, LANES), lambda i, jj, k: (i, 0))] * 3
            nq = mla_h * dq_c
            (q_c,) = _matmul(dqn, w_uq_p, w_map=tmap(j), n=nq, tm=tm, tn=_tile(nq, 512),
                             out=[(_tile(nq, 512), BF16, nq)],
                             epi=functools.partial(_epi_mla_q, seg=rope_dim // 4),
                             extra=list(tabs_c), extra_specs=tab_specs, name="uq")
            st_c, ckv_b, kr_b = _matmul(
                h, w_dkv_p, w_map=tmap(j), n=dkv_w, tm=tm, tn=dkv_w, tk=_tile(d, 2048),
                out=[(dkv_w, F32, dkv_w), (kv_lora, BF16, kv_lora), (LANES, BF16, LANES)],
                epi=functools.partial(_epi_mla_kv, kv_lora=kv_lora, seg=rope_dim // 4),
                extra=[kvn_c_g.reshape(-1, 1, kv_lora)] + list(tabs_c),
                extra_specs=[pl.BlockSpec((None, 1, kv_lora), lambda i, jj, k: (j, 0, 0))] + tab_specs, name="dkv")
            states_c.append(st_c[:n_p])
            c_all = jnp.concatenate([ckv_b, cache_ckv[:, j].reshape(nb_d * past, kv_lora).astype(BF16)], axis=0)
            kr_all = jnp.concatenate(
                [kr_b, jnp.pad(cache_krope[:, j].reshape(nb_d * past, rope_dim), ((0, 0), (0, LANES - rope_dim))).astype(BF16)],
                axis=0)
            n_kv_cols = w_ukv.shape[-1]
            (kv,) = _matmul(c_all, w_ukv, w_map=tmap(j), n=n_kv_cols, tm=_tile(c_all.shape[0], 1024),
                            tn=_tile(n_kv_cols, 1024), out=[(_tile(n_kv_cols, 1024), BF16, n_kv_cols)], name="ukv")
            scale = (nope + rope_dim) ** -0.5
            kcol, zero, vcol = (lambda hh: 2 * hh), (lambda hh: 0), (lambda hh: 2 * hh + 1)
            common = dict(k_cols=[kcol, zero], v_col=vcol, dq=dq_c, dv=v_dim, scale=scale, n_kv=mla_h, g=1)
            o_p = _attn_call(q_c, [kv, kr_all], kv, nb=nb_p, t=t_p, row0=0,
                             tq=_tile(t_p, 512), tk=_tile(t_p, 512), name="attn_c_p", **common)
            ctx_row = lambda b: nt // past + b
            ctx_specs = [pl.BlockSpec((past, LANES), lambda b, hh, i: (ctx_row(b), 2 * hh)),
                         pl.BlockSpec((past, LANES), lambda b, hh, i: (ctx_row(b), 0)),
                         pl.BlockSpec((past, LANES), lambda b, hh, i: (ctx_row(b), 2 * hh + 1))]
            o_d = _attn_call(q_c, [kv, kr_all], kv, nb=nb_d, t=t_d, row0=n_p,
                             tq=_tile(t_d, 512), tk=_tile(t_d, 512),
                             ctx=([kv, kr_all, kv], ctx_specs), name="attn_c_d", **common)
            o = jnp.concatenate([o_p, o_d], axis=0)
            (sub,) = _matmul(o, w_o_c, w_map=tmap(j), n=d, tm=tm, tn=_tile(d, 512), name="wo_c")

        ln_kw = dict(alpha=alpha, n_prompt=n_p, t_dec=t_d)
        if layer % 2 == 0:
            x, h = _ln_call(x, sub, mod4, ln1_g, ln1_b, layer=layer, gate_row=2, next_layer=layer, next_row=3, **ln_kw)
            act = _swiglu_call(h, w_ff_gate, w_ff_up, w_map=lambda jj: (j, 0, jj), n=d_ff, tm=tm, tn=_tile(d_ff, 256))
            (f,) = _matmul(act, w_ff_down, w_map=tmap(j), n=d, tm=tm, tn=_tile(d, 512), tk=_tile(d_ff, 4096), name="ff_down")
        else:
            x, h, comb = _ln_call(x, sub, mod4, ln1_g, ln1_b, layer=layer, gate_row=2, next_layer=layer, next_row=3,
                                  w_router=w_router_p[j], n_experts=n_e, **ln_kw)
            tn_e = _tile(ff_e, 256)
            tpe = ff_e // tn_e
            act = _swiglu_call(h, we_gate, we_up, w_map=lambda jj: (j, jj // tpe, 0, jj % tpe), n=n_e * ff_e,
                               tm=tm, tn=tn_e, comb=comb, tiles_per_expert=tpe)
            tk_e = _tile(ff_e, 2048)
            kpe = ff_e // tk_e
            (f,) = _matmul(act, we_down, w_map=lambda jj, k: (j, k // kpe, k % kpe, jj), n=d, tm=tm,
                           tn=_tile(d, 512), tk=tk_e, name="moe_down")
        if layer + 1 < depth:
            x, h = _ln_call(x, f, mod4, ln2_g, ln2_b, layer=layer, gate_row=5, next_layer=layer + 1, next_row=0, **ln_kw)
        else:
            (x,) = _ln_call(x, f, mod4, ln2_g, ln2_b, layer=layer, gate_row=5, **ln_kw)

    y_prompt = x[:n_p].reshape(nb_p, t_p, d)
    y_sample = x[n_p:].reshape(nb_d, t_d, d)
    st_ab = jnp.stack(states_ab, axis=0).reshape(len(states_ab), nb_p, t_p, -1)
    st_ab = jnp.transpose(st_ab, (1, 0, 2, 3))
    wa, wb = ka * hd, kb * hd
    new_k_a = st_ab[..., :wa].reshape(nb_p, -1, t_p, ka, hd)
    new_v_a = st_ab[..., wa:2 * wa].reshape(nb_p, -1, t_p, ka, hd)
    new_k_b = st_ab[..., 2 * wa:2 * wa + wb].reshape(nb_p, -1, t_p, kb, hd)
    new_v_b = st_ab[..., 2 * wa + wb:].reshape(nb_p, -1, t_p, kb, hd)
    st_c = jnp.transpose(jnp.stack(states_c, axis=0).reshape(len(states_c), nb_p, t_p, -1), (1, 0, 2, 3))
    new_ckv = st_c[..., :kv_lora]
    new_krope = st_c[..., kv_lora:kv_lora + rope_dim]
    return (y_prompt, y_sample, new_k_a, new_v_a, new_k_b, new_v_b, new_ckv, new_krope)
```

```python
import functools

import jax
import jax.numpy as jnp
from jax import lax
from jax.experimental import pallas as pl
from jax.experimental.pallas import tpu as pltpu

F32 = jnp.float32
BF16 = jnp.bfloat16

GRID_W = 64
WINDOW = 128
ROPE_THETA = 10000.0
EPS = 1e-6
NEG_INF = -1e30
TOP_K = 2
LANES = 128
LOG2E = 1.4426950408889634
VMEM_LIMIT_BYTES = 56 * 2**20


def _params(*sem):
    return pltpu.CompilerParams(dimension_semantics=sem, vmem_limit_bytes=VMEM_LIMIT_BYTES)


def _tile(n, pref):
    t = min(n, pref)
    while n % t:
        t //= 2
    return t


def _silu(x):
    return x / (1.0 + jnp.exp(-x))


def _mod_body(c_ref, w_ref, b_ref, o_ref):
    s = _silu(c_ref[...]).astype(BF16)
    o_ref[...] = jnp.dot(s, w_ref[...].astype(BF16), preferred_element_type=F32) + b_ref[...]


def _mod_call(cond, w_mod, b_mod):
    depth, d, n = w_mod.shape
    r = cond.shape[0]
    tn = _tile(n, 512)
    return pl.pallas_call(
        _mod_body,
        out_shape=jax.ShapeDtypeStruct((depth, r, n), F32),
        grid=(depth, n // tn),
        in_specs=[
            pl.BlockSpec((r, d), lambda l, j: (0, 0)),
            pl.BlockSpec((None, d, tn), lambda l, j: (l, 0, j)),
            pl.BlockSpec((None, 1, tn), lambda l, j: (l, 0, j)),
        ],
        out_specs=pl.BlockSpec((None, r, tn), lambda l, j: (l, 0, j)),
        compiler_params=_params("parallel", "parallel"),
        name="mod",
    )(cond, w_mod, b_mod.reshape(depth, 1, n))


def _cond_of_tile(i, tm, n_prompt, t_dec):
    np_tiles = n_prompt // tm
    return jnp.where(i < np_tiles, 0, 1 + (i - np_tiles) // (t_dec // tm))


def _mod_spec(layer, tm, n_prompt, t_dec, d):
    return pl.BlockSpec((None, None, 6, d), lambda i: (layer, _cond_of_tile(i, tm, n_prompt, t_dec), 0, 0))


def _modulate_body(x_ref, mod_ref, h_ref):
    sh = mod_ref[0:1, :]
    sc = mod_ref[1:2, :]
    h_ref[...] = (x_ref[...] * (1.0 + sc) + sh).astype(h_ref.dtype)


def _modulate_call(x, mod4, layer, n_prompt, t_dec):
    nt, d = x.shape
    tm = _tile(min(n_prompt, t_dec), 512)
    return pl.pallas_call(
        _modulate_body,
        out_shape=jax.ShapeDtypeStruct((nt, d), BF16),
        grid=(nt // tm,),
        in_specs=[pl.BlockSpec((tm, d), lambda i: (i, 0)), _mod_spec(layer, tm, n_prompt, t_dec, d)],
        out_specs=pl.BlockSpec((tm, d), lambda i: (i, 0)),
        compiler_params=_params("parallel"),
        name="modulate",
    )(x, mod4)


def _ln_body(*refs, alpha, gate_row, next_row, has_next, has_router, routed_sub, n_experts):
    it = iter(refs)
    x_ref, s_ref = next(it), next(it)
    s2_ref, rt_ref = (next(it), next(it)) if routed_sub else (None, None)
    modc_ref, g_ref, b_ref = next(it), next(it), next(it)
    modn_ref = next(it) if has_next else None
    wr_ref = next(it) if has_router else None
    xo_ref = next(it)
    h_ref = next(it) if has_next else None
    comb_ref = next(it) if has_router else None

    gate = modc_ref[gate_row:gate_row + 1, :]
    sub = s_ref[...]
    if routed_sub:
        sub = rt_ref[:, TOP_K:TOP_K + 1] * sub + rt_ref[:, TOP_K + 1:TOP_K + 2] * s2_ref[...]
    y = alpha * x_ref[...] + gate * sub
    mu = jnp.mean(y, axis=-1, keepdims=True)
    yc = y - mu
    var = jnp.mean(yc * yc, axis=-1, keepdims=True)
    xn = yc * lax.rsqrt(var + EPS) * g_ref[...] + b_ref[...]
    xo_ref[...] = xn
    if has_next:
        sh = modn_ref[next_row:next_row + 1, :]
        sc = modn_ref[next_row + 1:next_row + 2, :]
        h = xn * (1.0 + sc) + sh
        h_ref[...] = h.astype(h_ref.dtype)
    if has_router:
        logits = jnp.dot(h, wr_ref[...], preferred_element_type=F32, precision=lax.Precision.HIGHEST)
        lane = lax.broadcasted_iota(jnp.int32, logits.shape, 1)
        logits = jnp.where(lane < n_experts, logits, -jnp.inf)
        m1 = jnp.max(logits, axis=-1, keepdims=True)
        i1 = jnp.min(jnp.where(logits == m1, lane, LANES), axis=-1, keepdims=True)
        rest = jnp.where(lane == i1, -jnp.inf, logits)
        m2 = jnp.max(rest, axis=-1, keepdims=True)
        i2 = jnp.min(jnp.where(rest == m2, lane, LANES), axis=-1, keepdims=True)
        e2 = jnp.exp(m2 - m1)
        g1 = 1.0 / (1.0 + e2)
        g2 = e2 / (1.0 + e2)
        comb_ref[...] = (jnp.where(lane == 0, i1.astype(F32), 0.0) + jnp.where(lane == 1, i2.astype(F32), 0.0)
                         + jnp.where(lane == TOP_K, g1, 0.0) + jnp.where(lane == TOP_K + 1, g2, 0.0))


def _ln_call(x, sub, mod4, ln_g, ln_b, *, layer, gate_row, alpha, n_prompt, t_dec,
             next_layer=None, next_row=None, h_dtype=BF16, w_router=None, n_experts=0, route=None):
    nt, d = x.shape
    tm = _tile(min(n_prompt, t_dec), 256)
    has_next = next_layer is not None
    has_router = w_router is not None
    routed_sub = route is not None
    row = pl.BlockSpec((tm, d), lambda i: (i, 0))
    vec = pl.BlockSpec((None, 1, d), lambda i: (layer, 0, 0))
    in_specs = [row, row]
    args = [x, sub]
    if routed_sub:
        in_specs += [pl.BlockSpec((tm, d), lambda i: (nt // tm + i, 0)), pl.BlockSpec((tm, LANES), lambda i: (i, 0))]
        args += [sub, route]
    in_specs += [_mod_spec(layer, tm, n_prompt, t_dec, d), vec, vec]
    args += [mod4, ln_g.reshape(-1, 1, d), ln_b.reshape(-1, 1, d)]
    out_shape = [jax.ShapeDtypeStruct((nt, d), F32)]
    out_specs = [row]
    if has_next:
        in_specs.append(_mod_spec(next_layer, tm, n_prompt, t_dec, d))
        args.append(mod4)
        out_shape.append(jax.ShapeDtypeStruct((nt, d), h_dtype))
        out_specs.append(row)
    if has_router:
        in_specs.append(pl.BlockSpec((d, LANES), lambda i: (0, 0)))
        args.append(w_router)
        out_shape.append(jax.ShapeDtypeStruct((nt, LANES), F32))
        out_specs.append(pl.BlockSpec((tm, LANES), lambda i: (i, 0)))
    body = functools.partial(_ln_body, alpha=alpha, gate_row=gate_row, next_row=next_row, has_next=has_next,
                             has_router=has_router, routed_sub=routed_sub, n_experts=n_experts)
    return pl.pallas_call(
        body, out_shape=out_shape, grid=(nt // tm,), in_specs=in_specs, out_specs=out_specs,
        compiler_params=_params("parallel"), name="ln_mod",
    )(*args)


def _rms_rows(x, g):
    return x * lax.rsqrt(jnp.mean(x * x, axis=-1, keepdims=True) + EPS) * g


def _rope_lanes(x, c, s1, s2, seg):
    return x * c + pltpu.roll(x, LANES - seg, 1) * s1 + pltpu.roll(x, seg, 1) * s2


def _epi_none(acc, o_refs):
    o_refs[0][...] = acc.astype(o_refs[0].dtype)


def _epi_rms(acc, g_ref, o_refs):
    o_refs[0][...] = _rms_rows(acc, g_ref[...]).astype(o_refs[0].dtype)


def _epi_mla_q(acc, c_ref, s1_ref, s2_ref, o_refs, *, seg, qscale):
    n_heads = acc.shape[1] // (2 * LANES)
    acc = acc * qscale
    c, s1, s2 = c_ref[...], s1_ref[...], s2_ref[...]
    for h in range(n_heads):
        lo = h * 2 * LANES
        o_refs[0][:, lo:lo + LANES] = acc[:, lo:lo + LANES].astype(o_refs[0].dtype)
        o_refs[0][:, lo + LANES:lo + 2 * LANES] = _rope_lanes(
            acc[:, lo + LANES:lo + 2 * LANES], c, s1, s2, seg).astype(o_refs[0].dtype)


def _epi_mla_kv(acc, g_ref, c_ref, s1_ref, s2_ref, o_refs, *, kv_lora, seg):
    ckv = _rms_rows(acc[:, :kv_lora], g_ref[...])
    kr = acc[:, kv_lora:kv_lora + LANES]
    o_refs[0][:, :kv_lora] = ckv
    o_refs[0][:, kv_lora:kv_lora + LANES] = kr
    o_refs[1][...] = ckv.astype(o_refs[1].dtype)
    o_refs[2][...] = _rope_lanes(kr, c_ref[...], s1_ref[...], s2_ref[...], seg).astype(o_refs[2].dtype)


def _mm_body(*refs, nk, n_extra, n_out, epi):
    x_ref, w_ref = refs[0], refs[1]
    extra = refs[2:2 + n_extra]
    o_refs = refs[2 + n_extra:2 + n_extra + n_out]
    part = jnp.dot(x_ref[...], w_ref[...].astype(BF16), preferred_element_type=F32)
    if nk == 1:
        epi(part, *extra, o_refs)
    else:
        acc_ref = refs[2 + n_extra + n_out]
        k = pl.program_id(2)

        @pl.when(k == 0)
        def _():
            acc_ref[...] = part

        @pl.when(k > 0)
        def _():
            acc_ref[...] += part

        @pl.when(k == nk - 1)
        def _():
            epi(acc_ref[...], *extra, o_refs)


def _matmul(x, w, *, w_map, n, tm, tn, tk=None, out=None, epi=_epi_none, extra=(), extra_specs=(), name="mm"):
    m, kdim = x.shape
    tk = tk or kdim
    nk = kdim // tk
    if out is None:
        out = [(tn, F32, n)]
    w_block = (None,) * (w.ndim - 2) + (tk, tn)
    out_shape = [jax.ShapeDtypeStruct((m, width), dt) for (_, dt, width) in out]
    out_specs = [pl.BlockSpec((tm, bt), lambda i, j, k: (i, j)) for (bt, _, _) in out]
    body = functools.partial(_mm_body, nk=nk, n_extra=len(extra), n_out=len(out), epi=epi)
    res = pl.pallas_call(
        body,
        out_shape=out_shape,
        grid=(m // tm, n // tn, nk),
        in_specs=[pl.BlockSpec((tm, tk), lambda i, j, k: (i, k)),
                  pl.BlockSpec(w_block, lambda i, j, k: w_map(j, k))] + list(extra_specs),
        out_specs=out_specs,
        scratch_shapes=[pltpu.VMEM((tm, tn), F32)] if nk > 1 else [],
        compiler_params=_params("parallel", "parallel", "arbitrary"),
        name=name,
    )(x, w, *extra)
    return res


def _swiglu_tile(x_ref, wg_ref, wu_ref, o_ref):
    x = x_ref[...]
    g = jnp.dot(x, wg_ref[...].astype(BF16), preferred_element_type=F32)
    u = jnp.dot(x, wu_ref[...].astype(BF16), preferred_element_type=F32)
    o_ref[...] = (_silu(g) * u).astype(o_ref.dtype)


def _swiglu_call(x, wg, wu, *, w_map, n, tm, tn):
    m, kdim = x.shape
    w_block = (None,) * (wg.ndim - 2) + (kdim, tn)
    w_spec = pl.BlockSpec(w_block, lambda i, j: w_map(j))
    return pl.pallas_call(
        _swiglu_tile,
        out_shape=jax.ShapeDtypeStruct((m, n), BF16),
        grid=(m // tm, n // tn),
        in_specs=[pl.BlockSpec((tm, kdim), lambda i, j: (i, 0)), w_spec, w_spec],
        out_specs=pl.BlockSpec((tm, tn), lambda i, j: (i, j)),
        compiler_params=_params("parallel", "parallel"),
        name="swiglu",
    )(x, wg, wu)


def _gather_body(idx_ref, src_ref, o_ref, *scratch, tm, cast):
    if cast:
        buf_ref, sem = scratch
    else:
        (sem,) = scratch
        buf_ref = o_ref
    base = pl.program_id(0) * tm

    def issue(t, carry):
        pltpu.make_async_copy(src_ref.at[pl.ds(idx_ref[base + t], 1)], buf_ref.at[pl.ds(t, 1)], sem).start()
        return carry

    lax.fori_loop(0, tm, issue, 0)

    def drain(t, carry):
        pltpu.make_async_copy(src_ref.at[pl.ds(0, 1)], buf_ref.at[pl.ds(t, 1)], sem).wait()
        return carry

    lax.fori_loop(0, tm, drain, 0)
    if cast:
        o_ref[...] = buf_ref[...].astype(o_ref.dtype)


def _gather_rows(src, idx, *, tm, out_dtype):
    r = idx.shape[0]
    d = src.shape[1]
    cast = out_dtype != src.dtype
    scratch = ([pltpu.VMEM((tm, d), src.dtype)] if cast else []) + [pltpu.SemaphoreType.DMA(())]
    return pl.pallas_call(
        functools.partial(_gather_body, tm=tm, cast=cast),
        out_shape=jax.ShapeDtypeStruct((r, d), out_dtype),
        grid_spec=pltpu.PrefetchScalarGridSpec(
            num_scalar_prefetch=1,
            grid=(r // tm,),
            in_specs=[pl.BlockSpec(memory_space=pl.ANY)],
            out_specs=pl.BlockSpec((tm, d), lambda i, idx_ref: (i, 0)),
            scratch_shapes=scratch,
        ),
        compiler_params=_params("arbitrary"),
        name="gather_rows",
    )(idx, src)


def _routed_swiglu_body(te_ref, nu_ref, x_ref, wg_ref, wu_ref, o_ref):
    r = pl.program_id(1)

    @pl.when(r < nu_ref[0])
    def _():
        _swiglu_tile(x_ref, wg_ref, wu_ref, o_ref)

    @pl.when(r >= nu_ref[0])
    def _():
        o_ref[...] = jnp.zeros_like(o_ref)


def _routed_down_body(te_ref, nu_ref, x_ref, w_ref, o_ref):
    r = pl.program_id(1)

    @pl.when(r < nu_ref[0])
    def _():
        o_ref[...] = jnp.dot(x_ref[...], w_ref[...].astype(BF16), preferred_element_type=F32)

    @pl.when(r >= nu_ref[0])
    def _():
        o_ref[...] = jnp.zeros_like(o_ref)


def _routed_call(body, x, ws, tile_expert, n_used, *, layer, n, tm, tn, out_dtype, name):
    rows, kdim = x.shape
    used_row = lambda r, nu: jnp.minimum(r, nu[0] - 1)
    w_spec = pl.BlockSpec((None, None, kdim, tn), lambda j, r, te, nu: (layer, te[r], 0, j))
    return pl.pallas_call(
        body,
        out_shape=jax.ShapeDtypeStruct((rows, n), out_dtype),
        grid_spec=pltpu.PrefetchScalarGridSpec(
            num_scalar_prefetch=2,
            grid=(n // tn, rows // tm),
            in_specs=[pl.BlockSpec((tm, kdim), lambda j, r, te, nu: (used_row(r, nu), 0))] + [w_spec] * len(ws),
            out_specs=pl.BlockSpec((tm, tn), lambda j, r, te, nu: (r, j)),
        ),
        compiler_params=_params("parallel", "arbitrary"),
        name=name,
    )(tile_expert, n_used, x, *ws)


def _route_plan(route, n_experts, tm):
    nt = route.shape[0]
    picks = route[:, :TOP_K].astype(jnp.int32)
    flat = picks.T.reshape(-1)
    onehot = (flat[:, None] == jnp.arange(n_experts, dtype=jnp.int32)[None, :]).astype(jnp.int32)
    csum = jnp.cumsum(onehot, axis=0)
    rank = jnp.take_along_axis(csum, flat[:, None], axis=1)[:, 0] - 1
    counts = csum[-1]
    padded = (counts + tm - 1) // tm * tm
    ends = jnp.cumsum(padded)
    pos = (ends - padded)[flat] + rank
    n_rows = (TOP_K * nt // tm + n_experts) * tm
    tok = jnp.tile(jnp.arange(nt, dtype=jnp.int32), TOP_K)
    idx = jnp.zeros((n_rows,), jnp.int32).at[pos].set(tok)
    n_used = (ends[-1] // tm).astype(jnp.int32)
    tile_start = jnp.minimum(jnp.arange(n_rows // tm, dtype=jnp.int32), n_used - 1) * tm
    tile_expert = jnp.sum((tile_start[:, None] >= ends[None, :]).astype(jnp.int32), axis=1)
    return idx, pos.reshape(TOP_K, nt), tile_expert, n_used.reshape(1)


def _qkv_post_body(qkv_ref, qn_ref, kn_ref, c_ref, s1_ref, s2_ref,
                   qa_ref, ka_ref, va_ref, qb_ref, kb_ref, vb_ref, st_ref, *, ha, ka, hb, kb, seg, qscale):
    c, s1, s2 = c_ref[...], s1_ref[...], s2_ref[...]
    qn, kn = qn_ref[...], kn_ref[...]
    col = 0
    st = 0

    def head(idx):
        return qkv_ref[:, idx * LANES:(idx + 1) * LANES]

    for h in range(ha):
        q = _rms_rows(head(col + h), qn)
        qa_ref[:, h * LANES:(h + 1) * LANES] = (_rope_lanes(q, c, s1, s2, seg) * qscale).astype(qa_ref.dtype)
    col += ha
    for h in range(ka):
        k = _rms_rows(head(col + h), kn)
        st_ref[:, (st + h) * LANES:(st + h + 1) * LANES] = k
        ka_ref[:, h * LANES:(h + 1) * LANES] = _rope_lanes(k, c, s1, s2, seg).astype(ka_ref.dtype)
    col += ka
    st += ka
    for h in range(ka):
        v = head(col + h)
        st_ref[:, (st + h) * LANES:(st + h + 1) * LANES] = v
        va_ref[:, h * LANES:(h + 1) * LANES] = v.astype(va_ref.dtype)
    col += ka
    st += ka
    for h in range(hb):
        qb_ref[:, h * LANES:(h + 1) * LANES] = (_rope_lanes(head(col + h), c, s1, s2, seg) * qscale).astype(qb_ref.dtype)
    col += hb
    for h in range(kb):
        k = head(col + h)
        st_ref[:, (st + h) * LANES:(st + h + 1) * LANES] = k
        kb_ref[:, h * LANES:(h + 1) * LANES] = _rope_lanes(k, c, s1, s2, seg).astype(kb_ref.dtype)
    col += kb
    st += kb
    for h in range(kb):
        v = head(col + h)
        st_ref[:, (st + h) * LANES:(st + h + 1) * LANES] = v
        vb_ref[:, h * LANES:(h + 1) * LANES] = v.astype(vb_ref.dtype)


def _qkv_post_call(qkv, qn_g, kn_g, layer, tabs, *, ha, ka, hb, kb, seg):
    nt, width = qkv.shape
    tm = _tile(nt, 256)
    row = lambda w: pl.BlockSpec((tm, w), lambda i: (i, 0))
    gain = pl.BlockSpec((None, 1, LANES), lambda i: (layer, 0, 0))
    widths = [ha * LANES, ka * LANES, ka * LANES, hb * LANES, kb * LANES, kb * LANES]
    st_w = 2 * (ka + kb) * LANES
    body = functools.partial(_qkv_post_body, ha=ha, ka=ka, hb=hb, kb=kb, seg=seg, qscale=LANES ** -0.5 * LOG2E)
    return pl.pallas_call(
        body,
        out_shape=[jax.ShapeDtypeStruct((nt, w), BF16) for w in widths] + [jax.ShapeDtypeStruct((nt, st_w), F32)],
        grid=(nt // tm,),
        in_specs=[row(width), gain, gain, row(LANES), row(LANES), row(LANES)],
        out_specs=[row(w) for w in widths] + [row(st_w)],
        compiler_params=_params("parallel"),
        name="qkv_post",
    )(qkv, qn_g.reshape(-1, 1, LANES), kn_g.reshape(-1, 1, LANES), *tabs)


def _attn_body(*refs, g, dq, dv, tq, tk, t, n_kparts, window, has_ctx, has_sink):
    it = iter(refs)
    q_ref = next(it)
    k_refs = [next(it) for _ in range(n_kparts)]
    v_ref = next(it)
    kc_refs = [next(it) for _ in range(n_kparts)] if has_ctx else None
    vc_ref = next(it) if has_ctx else None
    sink_ref = next(it) if has_sink else None
    o_ref = next(it)

    h = pl.program_id(1)
    qi = pl.program_id(2)
    rows = g * tq
    q = jnp.concatenate([q_ref[:, i * dq:(i + 1) * dq] for i in range(g)], axis=0) if g > 1 else q_ref[...]

    def scores(parts):
        k = parts[0] if len(parts) == 1 else jnp.concatenate(parts, axis=1)
        return lax.dot_general(q, k.astype(BF16), (((1,), (1,)), ((), ())), preferred_element_type=F32)

    def update(carry, s, v):
        m, l, acc = carry
        m_new = jnp.maximum(m, jnp.max(s, axis=1, keepdims=True))
        a = jnp.exp2(m - m_new)
        p = jnp.exp2(s - m_new)
        l = a * l + jnp.sum(p, axis=1, keepdims=True)
        acc = a * acc + jnp.dot(p.astype(BF16), v.astype(BF16), preferred_element_type=F32)
        return m_new, l, acc

    carry = (jnp.full((rows, 1), NEG_INF, F32), jnp.zeros((rows, 1), F32), jnp.zeros((rows, dv), F32))
    if window is None:
        for c in range(t // tk):
            s = scores([r[c * tk:(c + 1) * tk, :] for r in k_refs])
            carry = update(carry, s, v_ref[c * tk:(c + 1) * tk, :])
    else:
        first = (qi * tq - window) // tk
        for r_ in range((tq + 2 * window) // tk):
            c = first + r_
            start = pl.multiple_of(jnp.clip(c, 0, t // tk - 1) * tk, tk)
            s = scores([r[pl.ds(start, tk), :] for r in k_refs])
            qpos = qi * tq + lax.rem(lax.broadcasted_iota(jnp.int32, s.shape, 0), tq)
            kpos = c * tk + lax.broadcasted_iota(jnp.int32, s.shape, 1)
            ok = (jnp.abs(qpos - kpos) <= window) & (kpos >= 0) & (kpos < t)
            carry = update(carry, jnp.where(ok, s, NEG_INF), v_ref[pl.ds(start, tk), :])
    if has_ctx:
        carry = update(carry, scores([r[...] for r in kc_refs]), vc_ref[...])
    m, l, acc = carry
    if has_sink:
        sk = jnp.concatenate([jnp.full((tq, 1), sink_ref[0, h * g + i] * LOG2E, F32) for i in range(g)], axis=0)
        m_new = jnp.maximum(m, sk)
        a = jnp.exp2(m - m_new)
        l = a * l + jnp.exp2(sk - m_new)
        acc = a * acc
    out = acc / l
    for i in range(g):
        o_ref[:, i * dv:(i + 1) * dv] = out[i * tq:(i + 1) * tq].astype(o_ref.dtype)


def _attn_call(q, kparts, v, *, nb, t, row0, n_kv, g, dq, dv, tq, tk, k_cols, v_col,
               window=None, ctx=None, sink=None, name="attn"):
    assert row0 % t == 0 and t % tq == 0 and t % tk == 0
    qt = t // tq
    in_specs = [pl.BlockSpec((tq, g * dq), lambda b, h, i: (row0 // tq + b * qt + i, h))]
    args = [q]
    for arr, col in zip(kparts, k_cols):
        in_specs.append(pl.BlockSpec((t, LANES), lambda b, h, i, col=col: (row0 // t + b, col(h))))
        args.append(arr)
    in_specs.append(pl.BlockSpec((t, dv), lambda b, h, i: (row0 // t + b, v_col(h))))
    args.append(v)
    if ctx is not None:
        args += list(ctx[0])
        in_specs += list(ctx[1])
    if sink is not None:
        args.append(sink)
        in_specs.append(pl.BlockSpec(memory_space=pltpu.SMEM))
    body = functools.partial(_attn_body, g=g, dq=dq, dv=dv, tq=tq, tk=tk, t=t, n_kparts=len(kparts),
                             window=window, has_ctx=ctx is not None, has_sink=sink is not None)
    return pl.pallas_call(
        body,
        out_shape=jax.ShapeDtypeStruct((nb * t, n_kv * g * dv), BF16),
        grid=(nb, n_kv, qt),
        in_specs=in_specs,
        out_specs=pl.BlockSpec((tq, g * dv), lambda b, h, i: (b * qt + i, h)),
        compiler_params=_params("parallel", "parallel", "arbitrary"),
        name=name,
    )(*args)


def _rope_tables(n_prompt, n_dec, t_dec, dim):
    seg = dim // 4
    d_axis = dim // 2
    row = jnp.repeat(jnp.arange(t_dec // GRID_W, dtype=F32), GRID_W)
    colp = (jnp.arange(t_dec) % GRID_W).astype(F32)
    inv = ROPE_THETA ** (-jnp.arange(0, d_axis, 2, dtype=F32) / d_axis)
    ar, ac = row[:, None] * inv, colp[:, None] * inv
    z = jnp.zeros_like(ar)
    pad = LANES - dim
    c = jnp.concatenate([jnp.cos(ar), jnp.cos(ar), jnp.cos(ac), jnp.cos(ac), jnp.ones((t_dec, pad), F32)], axis=1)
    s1 = jnp.concatenate([-jnp.sin(ar), z, -jnp.sin(ac), z, jnp.zeros((t_dec, pad), F32)], axis=1)
    s2 = jnp.concatenate([z, jnp.sin(ar), z, jnp.sin(ac), jnp.zeros((t_dec, pad), F32)], axis=1)

    def full(tab, fill):
        return jnp.concatenate([jnp.full((n_prompt, LANES), fill, F32)] + [tab] * n_dec, axis=0)

    return full(c, 1.0), full(s1, 0.0), full(s2, 0.0)


def kernel(x_prompt, x_sample, cache_k_a, cache_v_a, cache_k_b, cache_v_b, cache_ckv, cache_krope, c, c_ctx, w_mod, b_mod, ln1_g, ln1_b, ln2_g, ln2_b, w_qkv_ab, qn_a_g, kn_a_g, sink_b, w_o_ab, w_ff_gate, w_ff_up, w_ff_down, w_dq, qn_c_g, w_uq, w_dkv, kvn_c_g, w_ukv, w_o_c, w_router, we_gate, we_up, we_down):
    nb_p, t_p, d = x_prompt.shape
    nb_d, t_d, _ = x_sample.shape
    n_p, n_d = nb_p * t_p, nb_d * t_d
    nt = n_p + n_d
    depth = w_mod.shape[0]
    alpha = (2.0 * depth) ** 0.25
    past = cache_k_a.shape[2]
    ka, hd = cache_k_a.shape[3], cache_k_a.shape[4]
    kb = cache_k_b.shape[3]
    hb = sink_b.shape[1]
    ha = w_o_ab.shape[1] // hd - hb
    q_lora = w_dq.shape[-1]
    kv_lora = cache_ckv.shape[-1]
    rope_dim = cache_krope.shape[-1]
    n_e, ff_e = we_gate.shape[1], we_gate.shape[3]
    d_ff = w_ff_gate.shape[-1]
    mla_h = (w_uq.shape[-1] - w_ukv.shape[-1] + w_o_c.shape[1]) // rope_dim
    nope = w_uq.shape[-1] // mla_h - rope_dim
    v_dim = w_o_c.shape[1] // mla_h
    assert hd == LANES and nope == LANES and v_dim == LANES and rope_dim <= LANES
    assert nt % past == 0 and n_p % t_p == 0 and n_p % t_d == 0

    x = jnp.concatenate([x_prompt.reshape(n_p, d), x_sample.reshape(n_d, d)], axis=0)
    n_cond = 8
    cond = jnp.concatenate([c_ctx[None, :], c, jnp.zeros((n_cond - 1 - nb_d, d), F32)], axis=0)
    mod4 = _mod_call(cond, w_mod, b_mod).reshape(depth, n_cond, 6, d)

    tabs_ab = _rope_tables(n_p, nb_d, t_d, hd)
    tabs_c = _rope_tables(n_p, nb_d, t_d, rope_dim)

    tm = _tile(min(n_p, t_d), 1024)
    tmap = lambda layer: (lambda j, k: (layer, k, j))

    dq_c = 2 * LANES
    w_uq_p = jnp.pad(w_uq.reshape(w_uq.shape[0], q_lora, mla_h, nope + rope_dim),
                     ((0, 0), (0, 0), (0, 0), (0, dq_c - nope - rope_dim))).reshape(w_uq.shape[0], q_lora, mla_h * dq_c)
    dkv_w = kv_lora + LANES
    w_dkv_p = jnp.pad(w_dkv, ((0, 0), (0, 0), (0, dkv_w - w_dkv.shape[-1])))
    w_router_p = jnp.pad(w_router, ((0, 0), (0, 0), (0, LANES - n_e)))

    h = _modulate_call(x, mod4, 0, n_p, t_d)
    states_ab, states_c = [], []
    for layer in range(depth):
        j = layer // 2
        if layer % 2 == 0:
            n_qkv = w_qkv_ab.shape[-1]
            (qkv,) = _matmul(h, w_qkv_ab, w_map=tmap(j), n=n_qkv, tm=tm, tn=_tile(n_qkv, 512), name="qkv")
            qa, k_a, v_a, qb, k_b, v_b, st = _qkv_post_call(qkv, qn_a_g, kn_a_g, j, tabs_ab,
                                                            ha=ha, ka=ka, hb=hb, kb=kb, seg=hd // 4)
            states_ab.append(st[:n_p])
            ident = lambda hh: hh
            ck = lambda arr: arr.reshape(nb_d, arr.shape[1], past, -1)
            ctx_spec = pl.BlockSpec((None, None, past, LANES), lambda b, hh, i: (b, j, 0, hh))
            common = dict(k_cols=[ident], v_col=ident, dq=hd, dv=hd)
            oa_p = _attn_call(qa, [k_a], v_a, nb=nb_p, t=t_p, row0=0, n_kv=ka, g=ha // ka,
                              tq=_tile(t_p, 256), tk=_tile(t_p, 512), name="attn_a_p", **common)
            ob_p = _attn_call(qb, [k_b], v_b, nb=nb_p, t=t_p, row0=0, n_kv=kb, g=hb // kb,
                              tq=_tile(t_p, 256), tk=_tile(t_p, 512), sink=sink_b[j:j + 1], name="attn_b_p", **common)
            oa_d = _attn_call(qa, [k_a], v_a, nb=nb_d, t=t_d, row0=n_p, n_kv=ka, g=ha // ka,
                              tq=_tile(t_d, 256), tk=_tile(t_d, 512),
                              ctx=([ck(cache_k_a), ck(cache_v_a)], [ctx_spec, ctx_spec]), name="attn_a_d", **common)
            ob_d = _attn_call(qb, [k_b], v_b, nb=nb_d, t=t_d, row0=n_p, n_kv=kb, g=hb // kb,
                              tq=_tile(t_d, 256), tk=WINDOW, window=WINDOW,
                              ctx=([ck(cache_k_b), ck(cache_v_b)], [ctx_spec, ctx_spec]),
                              sink=sink_b[j:j + 1], name="attn_b_d", **common)
            o = jnp.concatenate([jnp.concatenate([oa_p, ob_p], axis=1), jnp.concatenate([oa_d, ob_d], axis=1)], axis=0)
            (sub,) = _matmul(o, w_o_ab, w_map=tmap(j), n=d, tm=tm, tn=_tile(d, 512), name="wo_ab")
        else:
            (dqn,) = _matmul(h, w_dq, w_map=tmap(j), n=q_lora, tm=tm, tn=q_lora, tk=_tile(d, 2048),
                             out=[(q_lora, BF16, q_lora)], epi=_epi_rms,
                             extra=[qn_c_g.reshape(-1, 1, q_lora)],
                             extra_specs=[pl.BlockSpec((None, 1, q_lora), lambda i, jj, k: (j, 0, 0))], name="dq")
            tab_specs = [pl.BlockSpec((tm, LANES), lambda i, jj, k: (i, 0))] * 3
            nq = mla_h * dq_c
            (q_c,) = _matmul(dqn, w_uq_p, w_map=tmap(j), n=nq, tm=tm, tn=_tile(nq, 512),
                             out=[(_tile(nq, 512), BF16, nq)],
                             epi=functools.partial(_epi_mla_q, seg=rope_dim // 4, qscale=(nope + rope_dim) ** -0.5 * LOG2E),
                             extra=list(tabs_c), extra_specs=tab_specs, name="uq")
            st_c, ckv_b, kr_b = _matmul(
                h, w_dkv_p, w_map=tmap(j), n=dkv_w, tm=tm, tn=dkv_w, tk=_tile(d, 2048),
                out=[(dkv_w, F32, dkv_w), (kv_lora, BF16, kv_lora), (LANES, BF16, LANES)],
                epi=functools.partial(_epi_mla_kv, kv_lora=kv_lora, seg=rope_dim // 4),
                extra=[kvn_c_g.reshape(-1, 1, kv_lora)] + list(tabs_c),
                extra_specs=[pl.BlockSpec((None, 1, kv_lora), lambda i, jj, k: (j, 0, 0))] + tab_specs, name="dkv")
            states_c.append(st_c[:n_p])
            c_all = jnp.concatenate([ckv_b, cache_ckv[:, j].reshape(nb_d * past, kv_lora).astype(BF16)], axis=0)
            kr_all = jnp.concatenate(
                [kr_b, jnp.pad(cache_krope[:, j].reshape(nb_d * past, rope_dim), ((0, 0), (0, LANES - rope_dim))).astype(BF16)],
                axis=0)
            n_kv_cols = w_ukv.shape[-1]
            (kv,) = _matmul(c_all, w_ukv, w_map=tmap(j), n=n_kv_cols, tm=_tile(c_all.shape[0], 1024),
                            tn=_tile(n_kv_cols, 1024), out=[(_tile(n_kv_cols, 1024), BF16, n_kv_cols)], name="ukv")
            kcol, zero, vcol = (lambda hh: 2 * hh), (lambda hh: 0), (lambda hh: 2 * hh + 1)
            common = dict(k_cols=[kcol, zero], v_col=vcol, dq=dq_c, dv=v_dim, n_kv=mla_h, g=1)
            o_p = _attn_call(q_c, [kv, kr_all], kv, nb=nb_p, t=t_p, row0=0,
                             tq=_tile(t_p, 512), tk=_tile(t_p, 512), name="attn_c_p", **common)
            ctx_row = lambda b: nt // past + b
            ctx_specs = [pl.BlockSpec((past, LANES), lambda b, hh, i: (ctx_row(b), 2 * hh)),
                         pl.BlockSpec((past, LANES), lambda b, hh, i: (ctx_row(b), 0)),
                         pl.BlockSpec((past, LANES), lambda b, hh, i: (ctx_row(b), 2 * hh + 1))]
            o_d = _attn_call(q_c, [kv, kr_all], kv, nb=nb_d, t=t_d, row0=n_p,
                             tq=_tile(t_d, 512), tk=_tile(t_d, 512),
                             ctx=([kv, kr_all, kv], ctx_specs), name="attn_c_d", **common)
            o = jnp.concatenate([o_p, o_d], axis=0)
            (sub,) = _matmul(o, w_o_c, w_map=tmap(j), n=d, tm=tm, tn=_tile(d, 512), name="wo_c")

        ln_kw = dict(alpha=alpha, n_prompt=n_p, t_dec=t_d)
        if layer % 2 == 0:
            x, h = _ln_call(x, sub, mod4, ln1_g, ln1_b, layer=layer, gate_row=2, next_layer=layer, next_row=3, **ln_kw)
            act = _swiglu_call(h, w_ff_gate, w_ff_up, w_map=lambda jj: (j, 0, jj), n=d_ff, tm=tm, tn=_tile(d_ff, 256))
            (f,) = _matmul(act, w_ff_down, w_map=tmap(j), n=d, tm=tm, tn=_tile(d, 512), tk=_tile(d_ff, 4096), name="ff_down")
        else:
            x, h, route = _ln_call(x, sub, mod4, ln1_g, ln1_b, layer=layer, gate_row=2, next_layer=layer, next_row=3,
                                   h_dtype=F32, w_router=w_router_p[j], n_experts=n_e, **ln_kw)
            tm_e = _tile(TOP_K * nt, 512)
            idx, pos, tile_expert, n_used = _route_plan(route, n_e, tm_e)
            xs = _gather_rows(h, idx, tm=tm_e, out_dtype=BF16)
            act = _routed_call(_routed_swiglu_body, xs, [we_gate, we_up], tile_expert, n_used, layer=j,
                               n=ff_e, tm=tm_e, tn=_tile(ff_e, 512), out_dtype=BF16, name="moe_swiglu")
            ys = _routed_call(_routed_down_body, act, [we_down], tile_expert, n_used, layer=j,
                              n=d, tm=tm_e, tn=_tile(d, 1024), out_dtype=F32, name="moe_down")
            f = _gather_rows(ys, pos.reshape(-1), tm=tm_e, out_dtype=F32)
            ln_kw = dict(route=route, **ln_kw)
        if layer + 1 < depth:
            x, h = _ln_call(x, f, mod4, ln2_g, ln2_b, layer=layer, gate_row=5, next_layer=layer + 1, next_row=0, **ln_kw)
        else:
            (x,) = _ln_call(x, f, mod4, ln2_g, ln2_b, layer=layer, gate_row=5, **ln_kw)

    y_prompt = x[:n_p].reshape(nb_p, t_p, d)
    y_sample = x[n_p:].reshape(nb_d, t_d, d)
    st_ab = jnp.stack(states_ab, axis=0).reshape(len(states_ab), nb_p, t_p, -1)
    st_ab = jnp.transpose(st_ab, (1, 0, 2, 3))
    wa, wb = ka * hd, kb * hd
    new_k_a = st_ab[..., :wa].reshape(nb_p, -1, t_p, ka, hd)
    new_v_a = st_ab[..., wa:2 * wa].reshape(nb_p, -1, t_p, ka, hd)
    new_k_b = st_ab[..., 2 * wa:2 * wa + wb].reshape(nb_p, -1, t_p, kb, hd)
    new_v_b = st_ab[..., 2 * wa + wb:].reshape(nb_p, -1, t_p, kb, hd)
    st_c = jnp.transpose(jnp.stack(states_c, axis=0).reshape(len(states_c), nb_p, t_p, -1), (1, 0, 2, 3))
    new_ckv = st_c[..., :kv_lora]
    new_krope = st_c[..., kv_lora:kv_lora + rope_dim]
    return (y_prompt, y_sample, new_k_a, new_v_a, new_k_b, new_v_b, new_ckv, new_krope)
```

```python
import functools

import jax
import jax.numpy as jnp
from jax import lax
from jax.experimental import pallas as pl
from jax.experimental.pallas import tpu as pltpu

F32 = jnp.float32
BF16 = jnp.bfloat16

GRID_W = 64
WINDOW = 128
ROPE_THETA = 10000.0
EPS = 1e-6
NEG_INF = -1e30
TOP_K = 2
LANES = 128
LOG2E = 1.4426950408889634
VMEM_LIMIT_BYTES = 56 * 2**20


def _params(*sem):
    return pltpu.CompilerParams(dimension_semantics=sem, vmem_limit_bytes=VMEM_LIMIT_BYTES)


def _tile(n, pref):
    t = min(n, pref)
    while n % t:
        t //= 2
    return t


def _silu(x):
    return x / (1.0 + jnp.exp(-x))


def _pack_bf16_halves(x):
    c = x.shape[1] // 2
    bits = pltpu.bitcast(x.astype(BF16).astype(F32), jnp.uint32)
    return (bits[:, :c] >> 16) | (bits[:, c:] & jnp.uint32(0xFFFF0000))


def _unpack_bf16_halves(w):
    lo = pltpu.bitcast(w << 16, F32).astype(BF16)
    hi = pltpu.bitcast(w & jnp.uint32(0xFFFF0000), F32).astype(BF16)
    return jnp.concatenate([lo, hi], axis=1)


def _mod_body(c_ref, w_ref, b_ref, o_ref):
    s = _silu(c_ref[...]).astype(BF16)
    o_ref[...] = jnp.dot(s, w_ref[...].astype(BF16), preferred_element_type=F32) + b_ref[...]


def _mod_call(cond, w_mod, b_mod):
    depth, d, n = w_mod.shape
    r = cond.shape[0]
    tn = _tile(n, 512)
    return pl.pallas_call(
        _mod_body,
        out_shape=jax.ShapeDtypeStruct((depth, r, n), F32),
        grid=(depth, n // tn),
        in_specs=[
            pl.BlockSpec((r, d), lambda l, j: (0, 0)),
            pl.BlockSpec((None, d, tn), lambda l, j: (l, 0, j)),
            pl.BlockSpec((None, 1, tn), lambda l, j: (l, 0, j)),
        ],
        out_specs=pl.BlockSpec((None, r, tn), lambda l, j: (l, 0, j)),
        compiler_params=_params("parallel", "parallel"),
        name="mod",
    )(cond, w_mod, b_mod.reshape(depth, 1, n))


def _cond_of_tile(i, tm, n_prompt, t_dec):
    np_tiles = n_prompt // tm
    return jnp.where(i < np_tiles, 0, 1 + (i - np_tiles) // (t_dec // tm))


def _mod_spec(layer, tm, n_prompt, t_dec, d):
    return pl.BlockSpec((None, None, 6, d), lambda i: (layer, _cond_of_tile(i, tm, n_prompt, t_dec), 0, 0))


def _modulate_body(x_ref, mod_ref, h_ref):
    sh = mod_ref[0:1, :]
    sc = mod_ref[1:2, :]
    h_ref[...] = (x_ref[...] * (1.0 + sc) + sh).astype(h_ref.dtype)


def _modulate_call(x, mod4, layer, n_prompt, t_dec):
    nt, d = x.shape
    tm = _tile(min(n_prompt, t_dec), 512)
    return pl.pallas_call(
        _modulate_body,
        out_shape=jax.ShapeDtypeStruct((nt, d), BF16),
        grid=(nt // tm,),
        in_specs=[pl.BlockSpec((tm, d), lambda i: (i, 0)), _mod_spec(layer, tm, n_prompt, t_dec, d)],
        out_specs=pl.BlockSpec((tm, d), lambda i: (i, 0)),
        compiler_params=_params("parallel"),
        name="modulate",
    )(x, mod4)


def _ln_body(*refs, alpha, gate_row, next_row, has_next, has_router, routed_sub, n_experts):
    it = iter(refs)
    x_ref, s_ref = next(it), next(it)
    s2_ref, rt_ref = (next(it), next(it)) if routed_sub else (None, None)
    modc_ref, g_ref, b_ref = next(it), next(it), next(it)
    modn_ref = next(it) if has_next else None
    wr_ref = next(it) if has_router else None
    xo_ref = next(it)
    h_ref = next(it) if has_next else None
    comb_ref = next(it) if has_router else None

    gate = modc_ref[gate_row:gate_row + 1, :]
    sub = s_ref[...]
    if routed_sub:
        sub = rt_ref[:, TOP_K:TOP_K + 1] * sub + rt_ref[:, TOP_K + 1:TOP_K + 2] * s2_ref[...]
    y = alpha * x_ref[...] + gate * sub
    mu = jnp.mean(y, axis=-1, keepdims=True)
    yc = y - mu
    var = jnp.mean(yc * yc, axis=-1, keepdims=True)
    xn = yc * lax.rsqrt(var + EPS) * g_ref[...] + b_ref[...]
    xo_ref[...] = xn
    if has_next:
        sh = modn_ref[next_row:next_row + 1, :]
        sc = modn_ref[next_row + 1:next_row + 2, :]
        h = xn * (1.0 + sc) + sh
        if h_ref.dtype == jnp.uint32:
            h_ref[...] = _pack_bf16_halves(h)
        else:
            h_ref[...] = h.astype(h_ref.dtype)
    if has_router:
        logits = jnp.dot(h, wr_ref[...], preferred_element_type=F32, precision=lax.Precision.HIGHEST)
        lane = lax.broadcasted_iota(jnp.int32, logits.shape, 1)
        logits = jnp.where(lane < n_experts, logits, -jnp.inf)
        m1 = jnp.max(logits, axis=-1, keepdims=True)
        i1 = jnp.min(jnp.where(logits == m1, lane, LANES), axis=-1, keepdims=True)
        rest = jnp.where(lane == i1, -jnp.inf, logits)
        m2 = jnp.max(rest, axis=-1, keepdims=True)
        i2 = jnp.min(jnp.where(rest == m2, lane, LANES), axis=-1, keepdims=True)
        e2 = jnp.exp(m2 - m1)
        g1 = 1.0 / (1.0 + e2)
        g2 = e2 / (1.0 + e2)
        comb_ref[...] = (jnp.where(lane == 0, i1.astype(F32), 0.0) + jnp.where(lane == 1, i2.astype(F32), 0.0)
                         + jnp.where(lane == TOP_K, g1, 0.0) + jnp.where(lane == TOP_K + 1, g2, 0.0))


def _ln_call(x, sub, mod4, ln_g, ln_b, *, layer, gate_row, alpha, n_prompt, t_dec,
             next_layer=None, next_row=None, h_dtype=BF16, w_router=None, n_experts=0, route=None):
    nt, d = x.shape
    tm = _tile(min(n_prompt, t_dec), 256)
    has_next = next_layer is not None
    has_router = w_router is not None
    routed_sub = route is not None
    row = pl.BlockSpec((tm, d), lambda i: (i, 0))
    vec = pl.BlockSpec((None, 1, d), lambda i: (layer, 0, 0))
    in_specs = [row, row]
    args = [x, sub]
    if routed_sub:
        in_specs += [pl.BlockSpec((tm, d), lambda i: (nt // tm + i, 0)), pl.BlockSpec((tm, LANES), lambda i: (i, 0))]
        args += [sub, route]
    in_specs += [_mod_spec(layer, tm, n_prompt, t_dec, d), vec, vec]
    args += [mod4, ln_g.reshape(-1, 1, d), ln_b.reshape(-1, 1, d)]
    out_shape = [jax.ShapeDtypeStruct((nt, d), F32)]
    out_specs = [row]
    if has_next:
        in_specs.append(_mod_spec(next_layer, tm, n_prompt, t_dec, d))
        args.append(mod4)
        h_cols = d // 2 if h_dtype == jnp.uint32 else d
        out_shape.append(jax.ShapeDtypeStruct((nt, h_cols), h_dtype))
        out_specs.append(pl.BlockSpec((tm, h_cols), lambda i: (i, 0)))
    if has_router:
        in_specs.append(pl.BlockSpec((d, LANES), lambda i: (0, 0)))
        args.append(w_router)
        out_shape.append(jax.ShapeDtypeStruct((nt, LANES), F32))
        out_specs.append(pl.BlockSpec((tm, LANES), lambda i: (i, 0)))
    body = functools.partial(_ln_body, alpha=alpha, gate_row=gate_row, next_row=next_row, has_next=has_next,
                             has_router=has_router, routed_sub=routed_sub, n_experts=n_experts)
    return pl.pallas_call(
        body, out_shape=out_shape, grid=(nt // tm,), in_specs=in_specs, out_specs=out_specs,
        compiler_params=_params("parallel"), name="ln_mod",
    )(*args)


def _rms_rows(x, g):
    return x * lax.rsqrt(jnp.mean(x * x, axis=-1, keepdims=True) + EPS) * g


def _rope_lanes(x, c, s1, s2, seg):
    return x * c + pltpu.roll(x, LANES - seg, 1) * s1 + pltpu.roll(x, seg, 1) * s2


def _epi_none(acc, o_refs):
    o_refs[0][...] = acc.astype(o_refs[0].dtype)


def _epi_rms(acc, g_ref, o_refs):
    o_refs[0][...] = _rms_rows(acc, g_ref[...]).astype(o_refs[0].dtype)


def _epi_mla_q(acc, c_ref, s1_ref, s2_ref, o_refs, *, seg, qscale):
    n_heads = acc.shape[1] // (2 * LANES)
    acc = acc * qscale
    c, s1, s2 = c_ref[...], s1_ref[...], s2_ref[...]
    for h in range(n_heads):
        lo = h * 2 * LANES
        o_refs[0][:, lo:lo + LANES] = acc[:, lo:lo + LANES].astype(o_refs[0].dtype)
        o_refs[0][:, lo + LANES:lo + 2 * LANES] = _rope_lanes(
            acc[:, lo + LANES:lo + 2 * LANES], c, s1, s2, seg).astype(o_refs[0].dtype)


def _epi_mla_kv(acc, g_ref, c_ref, s1_ref, s2_ref, o_refs, *, kv_lora, seg):
    ckv = _rms_rows(acc[:, :kv_lora], g_ref[...])
    kr = acc[:, kv_lora:kv_lora + LANES]
    o_refs[0][:, :kv_lora] = ckv
    o_refs[0][:, kv_lora:kv_lora + LANES] = kr
    o_refs[1][...] = ckv.astype(o_refs[1].dtype)
    o_refs[2][...] = _rope_lanes(kr, c_ref[...], s1_ref[...], s2_ref[...], seg).astype(o_refs[2].dtype)


def _mm_body(*refs, nk, n_extra, n_out, epi):
    x_ref, w_ref = refs[0], refs[1]
    extra = refs[2:2 + n_extra]
    o_refs = refs[2 + n_extra:2 + n_extra + n_out]
    part = jnp.dot(x_ref[...], w_ref[...].astype(BF16), preferred_element_type=F32)
    if nk == 1:
        epi(part, *extra, o_refs)
    else:
        acc_ref = refs[2 + n_extra + n_out]
        k = pl.program_id(2)

        @pl.when(k == 0)
        def _():
            acc_ref[...] = part

        @pl.when(k > 0)
        def _():
            acc_ref[...] += part

        @pl.when(k == nk - 1)
        def _():
            epi(acc_ref[...], *extra, o_refs)


def _matmul(x, w, *, w_map, n, tm, tn, tk=None, out=None, epi=_epi_none, extra=(), extra_specs=(), name="mm"):
    m, kdim = x.shape
    tk = tk or kdim
    nk = kdim // tk
    if out is None:
        out = [(tn, F32, n)]
    w_block = (None,) * (w.ndim - 2) + (tk, tn)
    out_shape = [jax.ShapeDtypeStruct((m, width), dt) for (_, dt, width) in out]
    out_specs = [pl.BlockSpec((tm, bt), lambda i, j, k: (i, j)) for (bt, _, _) in out]
    body = functools.partial(_mm_body, nk=nk, n_extra=len(extra), n_out=len(out), epi=epi)
    res = pl.pallas_call(
        body,
        out_shape=out_shape,
        grid=(m // tm, n // tn, nk),
        in_specs=[pl.BlockSpec((tm, tk), lambda i, j, k: (i, k)),
                  pl.BlockSpec(w_block, lambda i, j, k: w_map(j, k))] + list(extra_specs),
        out_specs=out_specs,
        scratch_shapes=[pltpu.VMEM((tm, tn), F32)] if nk > 1 else [],
        compiler_params=_params("parallel", "parallel", "arbitrary"),
        name=name,
    )(x, w, *extra)
    return res


def _swiglu_tile(x_ref, wg_ref, wu_ref, o_ref):
    x = x_ref[...]
    if x.dtype == jnp.uint32:
        x = _unpack_bf16_halves(x)
    g = jnp.dot(x, wg_ref[...].astype(BF16), preferred_element_type=F32)
    u = jnp.dot(x, wu_ref[...].astype(BF16), preferred_element_type=F32)
    o_ref[...] = (_silu(g) * u).astype(o_ref.dtype)


def _swiglu_call(x, wg, wu, *, w_map, n, tm, tn):
    m, kdim = x.shape
    w_block = (None,) * (wg.ndim - 2) + (kdim, tn)
    w_spec = pl.BlockSpec(w_block, lambda i, j: w_map(j))
    return pl.pallas_call(
        _swiglu_tile,
        out_shape=jax.ShapeDtypeStruct((m, n), BF16),
        grid=(m // tm, n // tn),
        in_specs=[pl.BlockSpec((tm, kdim), lambda i, j: (i, 0)), w_spec, w_spec],
        out_specs=pl.BlockSpec((tm, tn), lambda i, j: (i, j)),
        compiler_params=_params("parallel", "parallel"),
        name="swiglu",
    )(x, wg, wu)


GATHER_UNROLL = 8


def _gather_body(idx_ref, src_ref, o_ref, sem, *, tm):
    base = pl.program_id(0) * tm

    def issue(c, carry):
        for u in range(GATHER_UNROLL):
            t = c * GATHER_UNROLL + u
            pltpu.make_async_copy(src_ref.at[pl.ds(idx_ref[base + t], 1)], o_ref.at[pl.ds(t, 1)], sem).start()
        return carry

    lax.fori_loop(0, tm // GATHER_UNROLL, issue, 0)
    pltpu.make_async_copy(src_ref.at[pl.ds(0, tm)], o_ref, sem).wait()


def _gather_rows(src, idx, *, tm):
    r = idx.shape[0]
    d = src.shape[1]
    assert src.dtype.itemsize == 4 and tm % GATHER_UNROLL == 0
    return pl.pallas_call(
        functools.partial(_gather_body, tm=tm),
        out_shape=jax.ShapeDtypeStruct((r, d), src.dtype),
        grid_spec=pltpu.PrefetchScalarGridSpec(
            num_scalar_prefetch=1,
            grid=(r // tm,),
            in_specs=[pl.BlockSpec(memory_space=pl.ANY)],
            out_specs=pl.BlockSpec((tm, d), lambda i, idx_ref: (i, 0)),
            scratch_shapes=[pltpu.SemaphoreType.DMA(())],
        ),
        compiler_params=_params("arbitrary"),
        name="gather_rows",
    )(idx, src)


def _routed_swiglu_body(te_ref, nu_ref, x_ref, wg_ref, wu_ref, o_ref):
    r = pl.program_id(1)

    @pl.when(r < nu_ref[0])
    def _():
        _swiglu_tile(x_ref, wg_ref, wu_ref, o_ref)

    @pl.when(r >= nu_ref[0])
    def _():
        o_ref[...] = jnp.zeros_like(o_ref)


def _routed_down_body(te_ref, nu_ref, x_ref, w_ref, o_ref):
    r = pl.program_id(1)

    @pl.when(r < nu_ref[0])
    def _():
        o_ref[...] = jnp.dot(x_ref[...], w_ref[...].astype(BF16), preferred_element_type=F32)

    @pl.when(r >= nu_ref[0])
    def _():
        o_ref[...] = jnp.zeros_like(o_ref)


def _routed_call(body, x, ws, tile_expert, n_used, *, layer, n, tm, tn, out_dtype, name):
    rows, x_cols = x.shape
    kdim = ws[0].shape[-2]
    used_row = lambda r, nu: jnp.minimum(r, nu[0] - 1)
    w_spec = pl.BlockSpec((None, None, kdim, tn), lambda j, r, te, nu: (layer, te[r], 0, j))
    return pl.pallas_call(
        body,
        out_shape=jax.ShapeDtypeStruct((rows, n), out_dtype),
        grid_spec=pltpu.PrefetchScalarGridSpec(
            num_scalar_prefetch=2,
            grid=(n // tn, rows // tm),
            in_specs=[pl.BlockSpec((tm, x_cols), lambda j, r, te, nu: (used_row(r, nu), 0))] + [w_spec] * len(ws),
            out_specs=pl.BlockSpec((tm, tn), lambda j, r, te, nu: (r, j)),
        ),
        compiler_params=_params("parallel", "arbitrary"),
        name=name,
    )(tile_expert, n_used, x, *ws)


def _route_plan(route, n_experts, tm):
    nt = route.shape[0]
    picks = route[:, :TOP_K].astype(jnp.int32)
    flat = picks.T.reshape(-1)
    onehot = (flat[:, None] == jnp.arange(n_experts, dtype=jnp.int32)[None, :]).astype(jnp.int32)
    csum = jnp.cumsum(onehot, axis=0)
    rank = jnp.take_along_axis(csum, flat[:, None], axis=1)[:, 0] - 1
    counts = csum[-1]
    padded = (counts + tm - 1) // tm * tm
    ends = jnp.cumsum(padded)
    pos = (ends - padded)[flat] + rank
    n_rows = (TOP_K * nt // tm + n_experts) * tm
    tok = jnp.tile(jnp.arange(nt, dtype=jnp.int32), TOP_K)
    idx = jnp.zeros((n_rows,), jnp.int32).at[pos].set(tok)
    n_used = (ends[-1] // tm).astype(jnp.int32)
    tile_start = jnp.minimum(jnp.arange(n_rows // tm, dtype=jnp.int32), n_used - 1) * tm
    tile_expert = jnp.sum((tile_start[:, None] >= ends[None, :]).astype(jnp.int32), axis=1)
    return idx, pos.reshape(TOP_K, nt), tile_expert, n_used.reshape(1)


def _qkv_post_body(qkv_ref, qn_ref, kn_ref, c_ref, s1_ref, s2_ref,
                   qa_ref, ka_ref, va_ref, qb_ref, kb_ref, vb_ref, st_ref, *, ha, ka, hb, kb, seg, qscale):
    c, s1, s2 = c_ref[...], s1_ref[...], s2_ref[...]
    qn, kn = qn_ref[...], kn_ref[...]
    col = 0
    st = 0

    def head(idx):
        return qkv_ref[:, idx * LANES:(idx + 1) * LANES]

    for h in range(ha):
        q = _rms_rows(head(col + h), qn)
        qa_ref[:, h * LANES:(h + 1) * LANES] = (_rope_lanes(q, c, s1, s2, seg) * qscale).astype(qa_ref.dtype)
    col += ha
    for h in range(ka):
        k = _rms_rows(head(col + h), kn)
        st_ref[:, (st + h) * LANES:(st + h + 1) * LANES] = k
        ka_ref[:, h * LANES:(h + 1) * LANES] = _rope_lanes(k, c, s1, s2, seg).astype(ka_ref.dtype)
    col += ka
    st += ka
    for h in range(ka):
        v = head(col + h)
        st_ref[:, (st + h) * LANES:(st + h + 1) * LANES] = v
        va_ref[:, h * LANES:(h + 1) * LANES] = v.astype(va_ref.dtype)
    col += ka
    st += ka
    for h in range(hb):
        qb_ref[:, h * LANES:(h + 1) * LANES] = (_rope_lanes(head(col + h), c, s1, s2, seg) * qscale).astype(qb_ref.dtype)
    col += hb
    for h in range(kb):
        k = head(col + h)
        st_ref[:, (st + h) * LANES:(st + h + 1) * LANES] = k
        kb_ref[:, h * LANES:(h + 1) * LANES] = _rope_lanes(k, c, s1, s2, seg).astype(kb_ref.dtype)
    col += kb
    st += kb
    for h in range(kb):
        v = head(col + h)
        st_ref[:, (st + h) * LANES:(st + h + 1) * LANES] = v
        vb_ref[:, h * LANES:(h + 1) * LANES] = v.astype(vb_ref.dtype)


def _qkv_post_call(qkv, qn_g, kn_g, layer, tabs, *, ha, ka, hb, kb, seg):
    nt, width = qkv.shape
    tm = _tile(nt, 256)
    row = lambda w: pl.BlockSpec((tm, w), lambda i: (i, 0))
    gain = pl.BlockSpec((None, 1, LANES), lambda i: (layer, 0, 0))
    widths = [ha * LANES, ka * LANES, ka * LANES, hb * LANES, kb * LANES, kb * LANES]
    st_w = 2 * (ka + kb) * LANES
    body = functools.partial(_qkv_post_body, ha=ha, ka=ka, hb=hb, kb=kb, seg=seg, qscale=LANES ** -0.5 * LOG2E)
    return pl.pallas_call(
        body,
        out_shape=[jax.ShapeDtypeStruct((nt, w), BF16) for w in widths] + [jax.ShapeDtypeStruct((nt, st_w), F32)],
        grid=(nt // tm,),
        in_specs=[row(width), gain, gain, row(LANES), row(LANES), row(LANES)],
        out_specs=[row(w) for w in widths] + [row(st_w)],
        compiler_params=_params("parallel"),
        name="qkv_post",
    )(qkv, qn_g.reshape(-1, 1, LANES), kn_g.reshape(-1, 1, LANES), *tabs)


def _attn_body(*refs, hps, k_lanes, v_lane, **kw):
    n_kparts = len(k_lanes)
    it = iter(refs)
    q_ref = next(it)
    k_refs = [next(it) for _ in range(n_kparts)]
    v_ref = next(it)
    kc_refs = [next(it) for _ in range(n_kparts)] if kw["has_ctx"] else None
    vc_ref = next(it) if kw["has_ctx"] else None
    sink_ref = next(it) if kw["has_sink"] else None
    o_ref = refs[-1]
    for hh in range(hps):
        lanes = lambda ref, off: ref.at[:, off:off + LANES]
        _attn_head(q_ref, [lanes(r, f(hh)) for r, f in zip(k_refs, k_lanes)], lanes(v_ref, v_lane(hh)),
                   [lanes(r, f(hh)) for r, f in zip(kc_refs, k_lanes)] if kc_refs else None,
                   lanes(vc_ref, v_lane(hh)) if vc_ref is not None else None,
                   sink_ref, o_ref, hh=hh, hps=hps, **kw)


def _attn_head(q_ref, k_refs, v_ref, kc_refs, vc_ref, sink_ref, o_ref, *, hh, hps, g, dq, dv, tq, tk, t,
               window, has_ctx, has_sink):
    h = pl.program_id(1) * hps + hh
    qi = pl.program_id(2)
    rows = g * tq
    q0 = hh * g * dq
    q = (jnp.concatenate([q_ref[:, q0 + i * dq:q0 + (i + 1) * dq] for i in range(g)], axis=0) if g > 1
         else q_ref[:, q0:q0 + dq])

    def scores(parts):
        k = parts[0] if len(parts) == 1 else jnp.concatenate(parts, axis=1)
        return lax.dot_general(q, k.astype(BF16), (((1,), (1,)), ((), ())), preferred_element_type=F32)

    def update(carry, s, v):
        m, acc = carry
        m_new = jnp.maximum(m, jnp.max(s, axis=1, keepdims=True))
        a = jnp.exp2(m - m_new)
        p = jnp.exp2(s - m_new)
        v1 = jnp.concatenate([v.astype(BF16), jnp.ones((v.shape[0], LANES), BF16)], axis=1)
        acc = a * acc + jnp.dot(p.astype(BF16), v1, preferred_element_type=F32)
        return m_new, acc

    carry = (jnp.full((rows, 1), NEG_INF, F32), jnp.zeros((rows, dv + LANES), F32))
    if window is None:
        for c in range(t // tk):
            s = scores([r[c * tk:(c + 1) * tk, :] for r in k_refs])
            carry = update(carry, s, v_ref[c * tk:(c + 1) * tk, :])
    else:
        first = (qi * tq - window) // tk
        for r_ in range((tq + 2 * window) // tk):
            c = first + r_
            start = pl.multiple_of(jnp.clip(c, 0, t // tk - 1) * tk, tk)
            s = scores([r[pl.ds(start, tk), :] for r in k_refs])
            qpos = qi * tq + lax.rem(lax.broadcasted_iota(jnp.int32, s.shape, 0), tq)
            kpos = c * tk + lax.broadcasted_iota(jnp.int32, s.shape, 1)
            ok = (jnp.abs(qpos - kpos) <= window) & (kpos >= 0) & (kpos < t)
            carry = update(carry, jnp.where(ok, s, NEG_INF), v_ref[pl.ds(start, tk), :])
    if has_ctx:
        carry = update(carry, scores([r[...] for r in kc_refs]), vc_ref[...])
    m, acc = carry
    l = acc[:, dv:dv + 1]
    acc = acc[:, :dv]
    if has_sink:
        sk = jnp.concatenate([jnp.full((tq, 1), sink_ref[0, h * g + i] * LOG2E, F32) for i in range(g)], axis=0)
        m_new = jnp.maximum(m, sk)
        a = jnp.exp2(m - m_new)
        l = a * l + jnp.exp2(sk - m_new)
        acc = a * acc
    out = acc / l
    o0 = hh * g * dv
    for i in range(g):
        o_ref[:, o0 + i * dv:o0 + (i + 1) * dv] = out[i * tq:(i + 1) * tq].astype(o_ref.dtype)


def _attn_call(q, kparts, v, *, nb, t, row0, n_kv, g, dq, dv, tq, tk, hps, out_buf, out_col0,
               window=None, ctx=None, sink=None, name="attn"):
    assert row0 % t == 0 and t % tq == 0 and t % tk == 0 and out_col0 % (hps * g * dv) == 0 and n_kv % hps == 0
    qt = t // tq
    col_blk0 = out_col0 // (hps * g * dv)
    in_specs = [pl.BlockSpec((tq, hps * g * dq), lambda b, h, i: (row0 // tq + b * qt + i, h))]
    args = [q]

    def kv_spec(stride):
        if stride == 0:
            return pl.BlockSpec((t, LANES), lambda b, h, i: (row0 // t + b, 0))
        return pl.BlockSpec((t, hps * stride), lambda b, h, i: (row0 // t + b, h))

    lane_fn = lambda stride, off: (lambda hh: hh * stride + off)
    for arr, stride, off in kparts:
        in_specs.append(kv_spec(stride))
        args.append(arr)
    in_specs.append(kv_spec(v[1]))
    args.append(v[0])
    if ctx is not None:
        args += list(ctx[0])
        in_specs += list(ctx[1])
    if sink is not None:
        args.append(sink)
        in_specs.append(pl.BlockSpec(memory_space=pltpu.SMEM))
    args.append(out_buf)
    in_specs.append(pl.BlockSpec(memory_space=pl.ANY))
    body = functools.partial(_attn_body, hps=hps, k_lanes=[lane_fn(s_, o_) for _, s_, o_ in kparts],
                             v_lane=lane_fn(v[1], v[2]), g=g, dq=dq, dv=dv, tq=tq, tk=tk, t=t,
                             window=window, has_ctx=ctx is not None, has_sink=sink is not None)
    return pl.pallas_call(
        body,
        out_shape=jax.ShapeDtypeStruct(out_buf.shape, out_buf.dtype),
        grid=(nb, n_kv // hps, qt),
        in_specs=in_specs,
        out_specs=pl.BlockSpec((tq, hps * g * dv), lambda b, h, i: (row0 // tq + b * qt + i, col_blk0 + h)),
        input_output_aliases={len(args) - 1: 0},
        compiler_params=_params("parallel", "parallel", "arbitrary"),
        name=name,
    )(*args)


def _rope_tables(n_prompt, n_dec, t_dec, dim):
    seg = dim // 4
    d_axis = dim // 2
    row = jnp.repeat(jnp.arange(t_dec // GRID_W, dtype=F32), GRID_W)
    colp = (jnp.arange(t_dec) % GRID_W).astype(F32)
    inv = ROPE_THETA ** (-jnp.arange(0, d_axis, 2, dtype=F32) / d_axis)
    ar, ac = row[:, None] * inv, colp[:, None] * inv
    z = jnp.zeros_like(ar)
    pad = LANES - dim
    c = jnp.concatenate([jnp.cos(ar), jnp.cos(ar), jnp.cos(ac), jnp.cos(ac), jnp.ones((t_dec, pad), F32)], axis=1)
    s1 = jnp.concatenate([-jnp.sin(ar), z, -jnp.sin(ac), z, jnp.zeros((t_dec, pad), F32)], axis=1)
    s2 = jnp.concatenate([z, jnp.sin(ar), z, jnp.sin(ac), jnp.zeros((t_dec, pad), F32)], axis=1)

    def full(tab, fill):
        return jnp.concatenate([jnp.full((n_prompt, LANES), fill, F32)] + [tab] * n_dec, axis=0)

    return full(c, 1.0), full(s1, 0.0), full(s2, 0.0)


def kernel(x_prompt, x_sample, cache_k_a, cache_v_a, cache_k_b, cache_v_b, cache_ckv, cache_krope, c, c_ctx, w_mod, b_mod, ln1_g, ln1_b, ln2_g, ln2_b, w_qkv_ab, qn_a_g, kn_a_g, sink_b, w_o_ab, w_ff_gate, w_ff_up, w_ff_down, w_dq, qn_c_g, w_uq, w_dkv, kvn_c_g, w_ukv, w_o_c, w_router, we_gate, we_up, we_down):
    nb_p, t_p, d = x_prompt.shape
    nb_d, t_d, _ = x_sample.shape
    n_p, n_d = nb_p * t_p, nb_d * t_d
    nt = n_p + n_d
    depth = w_mod.shape[0]
    alpha = (2.0 * depth) ** 0.25
    past = cache_k_a.shape[2]
    ka, hd = cache_k_a.shape[3], cache_k_a.shape[4]
    kb = cache_k_b.shape[3]
    hb = sink_b.shape[1]
    ha = w_o_ab.shape[1] // hd - hb
    q_lora = w_dq.shape[-1]
    kv_lora = cache_ckv.shape[-1]
    rope_dim = cache_krope.shape[-1]
    n_e, ff_e = we_gate.shape[1], we_gate.shape[3]
    d_ff = w_ff_gate.shape[-1]
    mla_h = (w_uq.shape[-1] - w_ukv.shape[-1] + w_o_c.shape[1]) // rope_dim
    nope = w_uq.shape[-1] // mla_h - rope_dim
    v_dim = w_o_c.shape[1] // mla_h
    assert hd == LANES and nope == LANES and v_dim == LANES and rope_dim <= LANES
    assert nt % past == 0 and n_p % t_p == 0 and n_p % t_d == 0

    x = jnp.concatenate([x_prompt.reshape(n_p, d), x_sample.reshape(n_d, d)], axis=0)
    n_cond = 8
    cond = jnp.concatenate([c_ctx[None, :], c, jnp.zeros((n_cond - 1 - nb_d, d), F32)], axis=0)
    mod4 = _mod_call(cond, w_mod, b_mod).reshape(depth, n_cond, 6, d)

    tabs_ab = _rope_tables(n_p, nb_d, t_d, hd)
    tabs_c = _rope_tables(n_p, nb_d, t_d, rope_dim)

    tm = _tile(min(n_p, t_d), 1024)
    tmap = lambda layer: (lambda j, k: (layer, k, j))

    dq_c = 2 * LANES
    w_uq_p = jnp.pad(w_uq.reshape(w_uq.shape[0], q_lora, mla_h, nope + rope_dim),
                     ((0, 0), (0, 0), (0, 0), (0, dq_c - nope - rope_dim))).reshape(w_uq.shape[0], q_lora, mla_h * dq_c)
    dkv_w = kv_lora + LANES
    w_dkv_p = jnp.pad(w_dkv, ((0, 0), (0, 0), (0, dkv_w - w_dkv.shape[-1])))
    w_router_p = jnp.pad(w_router, ((0, 0), (0, 0), (0, LANES - n_e)))

    h = _modulate_call(x, mod4, 0, n_p, t_d)
    states_ab, states_c = [], []
    for layer in range(depth):
        j = layer // 2
        if layer % 2 == 0:
            n_qkv = w_qkv_ab.shape[-1]
            (qkv,) = _matmul(h, w_qkv_ab, w_map=tmap(j), n=n_qkv, tm=tm, tn=_tile(n_qkv, 512), name="qkv")
            qa, k_a, v_a, qb, k_b, v_b, st = _qkv_post_call(qkv, qn_a_g, kn_a_g, j, tabs_ab,
                                                            ha=ha, ka=ka, hb=hb, kb=kb, seg=hd // 4)
            states_ab.append(st[:n_p])
            ck = lambda arr: arr.reshape(nb_d, arr.shape[1], past, -1)
            ctx_spec = pl.BlockSpec((None, None, past, LANES), lambda b, hh, i: (b, j, 0, hh))
            common = dict(dq=hd, dv=hd)
            per_head = lambda arr: (arr, LANES, 0)
            o = jnp.zeros((nt, (ha + hb) * hd), BF16)
            o = _attn_call(qa, [per_head(k_a)], per_head(v_a), nb=nb_p, t=t_p, row0=0, n_kv=ka, g=ha // ka, hps=1,
                           out_buf=o, out_col0=0, tq=_tile(t_p, 256), tk=_tile(t_p, 512), name="attn_a_p", **common)
            o = _attn_call(qb, [per_head(k_b)], per_head(v_b), nb=nb_p, t=t_p, row0=0, n_kv=kb, g=hb // kb, hps=kb,
                           out_buf=o, out_col0=ha * hd, tq=_tile(t_p, 256), tk=_tile(t_p, 512),
                           sink=sink_b[j:j + 1], name="attn_b_p", **common)
            o = _attn_call(qa, [per_head(k_a)], per_head(v_a), nb=nb_d, t=t_d, row0=n_p, n_kv=ka, g=ha // ka, hps=1,
                           out_buf=o, out_col0=0, tq=_tile(t_d, 256), tk=_tile(t_d, 1024),
                           ctx=([ck(cache_k_a), ck(cache_v_a)], [ctx_spec, ctx_spec]), name="attn_a_d", **common)
            o = _attn_call(qb, [per_head(k_b)], per_head(v_b), nb=nb_d, t=t_d, row0=n_p, n_kv=kb, g=hb // kb, hps=1,
                           out_buf=o, out_col0=ha * hd, tq=_tile(t_d, 256), tk=WINDOW, window=WINDOW,
                           ctx=([ck(cache_k_b), ck(cache_v_b)], [ctx_spec, ctx_spec]),
                           sink=sink_b[j:j + 1], name="attn_b_d", **common)
            (sub,) = _matmul(o, w_o_ab, w_map=tmap(j), n=d, tm=tm, tn=_tile(d, 512), name="wo_ab")
        else:
            (dqn,) = _matmul(h, w_dq, w_map=tmap(j), n=q_lora, tm=tm, tn=q_lora, tk=_tile(d, 2048),
                             out=[(q_lora, BF16, q_lora)], epi=_epi_rms,
                             extra=[qn_c_g.reshape(-1, 1, q_lora)],
                             extra_specs=[pl.BlockSpec((None, 1, q_lora), lambda i, jj, k: (j, 0, 0))], name="dq")
            tab_specs = [pl.BlockSpec((tm, LANES), lambda i, jj, k: (i, 0))] * 3
            nq = mla_h * dq_c
            (q_c,) = _matmul(dqn, w_uq_p, w_map=tmap(j), n=nq, tm=tm, tn=_tile(nq, 512),
                             out=[(_tile(nq, 512), BF16, nq)],
                             epi=functools.partial(_epi_mla_q, seg=rope_dim // 4, qscale=(nope + rope_dim) ** -0.5 * LOG2E),
                             extra=list(tabs_c), extra_specs=tab_specs, name="uq")
            st_c, ckv_b, kr_b = _matmul(
                h, w_dkv_p, w_map=tmap(j), n=dkv_w, tm=tm, tn=dkv_w, tk=_tile(d, 2048),
                out=[(dkv_w, F32, dkv_w), (kv_lora, BF16, kv_lora), (LANES, BF16, LANES)],
                epi=functools.partial(_epi_mla_kv, kv_lora=kv_lora, seg=rope_dim // 4),
                extra=[kvn_c_g.reshape(-1, 1, kv_lora)] + list(tabs_c),
                extra_specs=[pl.BlockSpec((None, 1, kv_lora), lambda i, jj, k: (j, 0, 0))] + tab_specs, name="dkv")
            states_c.append(st_c[:n_p])
            c_all = jnp.concatenate([ckv_b, cache_ckv[:, j].reshape(nb_d * past, kv_lora).astype(BF16)], axis=0)
            kr_all = jnp.concatenate(
                [kr_b, jnp.pad(cache_krope[:, j].reshape(nb_d * past, rope_dim), ((0, 0), (0, LANES - rope_dim))).astype(BF16)],
                axis=0)
            n_kv_cols = w_ukv.shape[-1]
            (kv,) = _matmul(c_all, w_ukv, w_map=tmap(j), n=n_kv_cols, tm=_tile(c_all.shape[0], 1024),
                            tn=_tile(n_kv_cols, 1024), out=[(_tile(n_kv_cols, 1024), BF16, n_kv_cols)], name="ukv")
            kv_w = 2 * LANES
            k_parts, v_part = [(kv, kv_w, 0), (kr_all, 0, 0)], (kv, kv_w, LANES)
            common = dict(dq=dq_c, dv=v_dim, n_kv=mla_h, g=1)
            o = jnp.zeros((nt, mla_h * v_dim), BF16)
            o = _attn_call(q_c, k_parts, v_part, nb=nb_p, t=t_p, row0=0, hps=_tile(mla_h, 8), out_buf=o, out_col0=0,
                           tq=_tile(t_p, 512), tk=_tile(t_p, 512), name="attn_c_p", **common)
            ctx_row = lambda b: nt // past + b
            ctx_specs = [pl.BlockSpec((past, kv_w), lambda b, hh, i: (ctx_row(b), hh)),
                         pl.BlockSpec((past, LANES), lambda b, hh, i: (ctx_row(b), 0)),
                         pl.BlockSpec((past, kv_w), lambda b, hh, i: (ctx_row(b), hh))]
            o = _attn_call(q_c, k_parts, v_part, nb=nb_d, t=t_d, row0=n_p, hps=1, out_buf=o, out_col0=0,
                           tq=_tile(t_d, 1024), tk=_tile(t_d, 1024),
                           ctx=([kv, kr_all, kv], ctx_specs), name="attn_c_d", **common)
            (sub,) = _matmul(o, w_o_c, w_map=tmap(j), n=d, tm=tm, tn=_tile(d, 512), name="wo_c")

        ln_kw = dict(alpha=alpha, n_prompt=n_p, t_dec=t_d)
        if layer % 2 == 0:
            x, h = _ln_call(x, sub, mod4, ln1_g, ln1_b, layer=layer, gate_row=2, next_layer=layer, next_row=3, **ln_kw)
            act = _swiglu_call(h, w_ff_gate, w_ff_up, w_map=lambda jj: (j, 0, jj), n=d_ff, tm=tm, tn=_tile(d_ff, 256))
            (f,) = _matmul(act, w_ff_down, w_map=tmap(j), n=d, tm=tm, tn=_tile(d, 512), tk=_tile(d_ff, 4096), name="ff_down")
        else:
            x, h, route = _ln_call(x, sub, mod4, ln1_g, ln1_b, layer=layer, gate_row=2, next_layer=layer, next_row=3,
                                   h_dtype=jnp.uint32, w_router=w_router_p[j], n_experts=n_e, **ln_kw)
            tm_e = _tile(TOP_K * nt, 512)
            idx, pos, tile_expert, n_used = _route_plan(route, n_e, tm_e)
            xs = _gather_rows(h, idx, tm=tm_e)
            act = _routed_call(_routed_swiglu_body, xs, [we_gate, we_up], tile_expert, n_used, layer=j,
                               n=ff_e, tm=tm_e, tn=_tile(ff_e, 512), out_dtype=BF16, name="moe_swiglu")
            ys = _routed_call(_routed_down_body, act, [we_down], tile_expert, n_used, layer=j,
                              n=d, tm=tm_e, tn=_tile(d, 1024), out_dtype=F32, name="moe_down")
            f = _gather_rows(ys, pos.reshape(-1), tm=tm_e)
            ln_kw = dict(route=route, **ln_kw)
        if layer + 1 < depth:
            x, h = _ln_call(x, f, mod4, ln2_g, ln2_b, layer=layer, gate_row=5, next_layer=layer + 1, next_row=0, **ln_kw)
        else:
            (x,) = _ln_call(x, f, mod4, ln2_g, ln2_b, layer=layer, gate_row=5, **ln_kw)

    y_prompt = x[:n_p].reshape(nb_p, t_p, d)
    y_sample = x[n_p:].reshape(nb_d, t_d, d)
    st_ab = jnp.stack(states_ab, axis=0).reshape(len(states_ab), nb_p, t_p, -1)
    st_ab = jnp.transpose(st_ab, (1, 0, 2, 3))
    wa, wb = ka * hd, kb * hd
    new_k_a = st_ab[..., :wa].reshape(nb_p, -1, t_p, ka, hd)
    new_v_a = st_ab[..., wa:2 * wa].reshape(nb_p, -1, t_p, ka, hd)
    new_k_b = st_ab[..., 2 * wa:2 * wa + wb].reshape(nb_p, -1, t_p, kb, hd)
    new_v_b = st_ab[..., 2 * wa + wb:].reshape(nb_p, -1, t_p, kb, hd)
    st_c = jnp.transpose(jnp.stack(states_c, axis=0).reshape(len(states_c), nb_p, t_p, -1), (1, 0, 2, 3))
    new_ckv = st_c[..., :kv_lora]
    new_krope = st_c[..., kv_lora:kv_lora + rope_dim]
    return (y_prompt, y_sample, new_k_a, new_v_a, new_k_b, new_v_b, new_ckv, new_krope)
```

```python
import functools

import jax
import jax.numpy as jnp
from jax import lax
from jax.experimental import pallas as pl
from jax.experimental.pallas import tpu as pltpu

F32 = jnp.float32
BF16 = jnp.bfloat16

GRID_W = 64
WINDOW = 128
ROPE_THETA = 10000.0
EPS = 1e-6
NEG_INF = -1e30
TOP_K = 2
LANES = 128
LOG2E = 1.4426950408889634
VMEM_LIMIT_BYTES = 56 * 2**20


def _params(*sem):
    return pltpu.CompilerParams(dimension_semantics=sem, vmem_limit_bytes=VMEM_LIMIT_BYTES)


def _tile(n, pref):
    t = min(n, pref)
    while n % t:
        t //= 2
    return t


def _silu(x):
    return x / (1.0 + jnp.exp(-x))


def _pack_bf16_halves(x):
    c = x.shape[1] // 2
    bits = pltpu.bitcast(x.astype(BF16).astype(F32), jnp.uint32)
    return (bits[:, :c] >> 16) | (bits[:, c:] & jnp.uint32(0xFFFF0000))


def _unpack_bf16_halves(w):
    lo = pltpu.bitcast(w << 16, F32).astype(BF16)
    hi = pltpu.bitcast(w & jnp.uint32(0xFFFF0000), F32).astype(BF16)
    return jnp.concatenate([lo, hi], axis=1)


def _mod_body(c_ref, w_ref, b_ref, o_ref):
    s = _silu(c_ref[...]).astype(BF16)
    o_ref[...] = jnp.dot(s, w_ref[...].astype(BF16), preferred_element_type=F32) + b_ref[...]


def _mod_call(cond, w_mod, b_mod):
    depth, d, n = w_mod.shape
    r = cond.shape[0]
    tn = _tile(n, 512)
    return pl.pallas_call(
        _mod_body,
        out_shape=jax.ShapeDtypeStruct((depth, r, n), F32),
        grid=(depth, n // tn),
        in_specs=[
            pl.BlockSpec((r, d), lambda l, j: (0, 0)),
            pl.BlockSpec((None, d, tn), lambda l, j: (l, 0, j)),
            pl.BlockSpec((None, 1, tn), lambda l, j: (l, 0, j)),
        ],
        out_specs=pl.BlockSpec((None, r, tn), lambda l, j: (l, 0, j)),
        compiler_params=_params("parallel", "parallel"),
        name="mod",
    )(cond, w_mod, b_mod.reshape(depth, 1, n))


def _cond_of_tile(i, tm, n_prompt, t_dec):
    np_tiles = n_prompt // tm
    return jnp.where(i < np_tiles, 0, 1 + (i - np_tiles) // (t_dec // tm))


def _mod_spec(layer, tm, n_prompt, t_dec, d, tile0=0):
    return pl.BlockSpec((None, None, 6, d), lambda i: (layer, _cond_of_tile(i + tile0, tm, n_prompt, t_dec), 0, 0))


def _modulate_body(x_ref, mod_ref, h_ref):
    sh = mod_ref[0:1, :]
    sc = mod_ref[1:2, :]
    h_ref[...] = (x_ref[...] * (1.0 + sc) + sh).astype(h_ref.dtype)


def _modulate_call(x, mod4, layer, n_prompt, t_dec):
    nt, d = x.shape
    tm = _tile(min(n_prompt, t_dec), 512)
    return pl.pallas_call(
        _modulate_body,
        out_shape=jax.ShapeDtypeStruct((nt, d), BF16),
        grid=(nt // tm,),
        in_specs=[pl.BlockSpec((tm, d), lambda i: (i, 0)), _mod_spec(layer, tm, n_prompt, t_dec, d)],
        out_specs=pl.BlockSpec((tm, d), lambda i: (i, 0)),
        compiler_params=_params("parallel"),
        name="modulate",
    )(x, mod4)


def _ln_body(*refs, alpha, gate_row, next_row, has_next, has_router, routed_sub, n_experts):
    it = iter(refs)
    x_ref, s_ref = next(it), next(it)
    s2_ref, rt_ref = (next(it), next(it)) if routed_sub else (None, None)
    modc_ref, g_ref, b_ref = next(it), next(it), next(it)
    modn_ref = next(it) if has_next else None
    wr_ref = next(it) if has_router else None
    xo_ref = next(it)
    h_ref = next(it) if has_next else None
    comb_ref = next(it) if has_router else None

    gate = modc_ref[gate_row:gate_row + 1, :]
    sub = s_ref[...]
    if routed_sub:
        sub = rt_ref[:, TOP_K:TOP_K + 1] * sub + rt_ref[:, TOP_K + 1:TOP_K + 2] * s2_ref[...]
    y = alpha * x_ref[...] + gate * sub
    mu = jnp.mean(y, axis=-1, keepdims=True)
    yc = y - mu
    var = jnp.mean(yc * yc, axis=-1, keepdims=True)
    xn = yc * lax.rsqrt(var + EPS) * g_ref[...] + b_ref[...]
    xo_ref[...] = xn
    if has_next:
        sh = modn_ref[next_row:next_row + 1, :]
        sc = modn_ref[next_row + 1:next_row + 2, :]
        h = xn * (1.0 + sc) + sh
        if h_ref.dtype == jnp.uint32:
            h_ref[...] = _pack_bf16_halves(h)
        else:
            h_ref[...] = h.astype(h_ref.dtype)
    if has_router:
        logits = jnp.dot(h, wr_ref[...], preferred_element_type=F32, precision=lax.Precision.HIGHEST)
        lane = lax.broadcasted_iota(jnp.int32, logits.shape, 1)
        logits = jnp.where(lane < n_experts, logits, -jnp.inf)
        m1 = jnp.max(logits, axis=-1, keepdims=True)
        i1 = jnp.min(jnp.where(logits == m1, lane, LANES), axis=-1, keepdims=True)
        rest = jnp.where(lane == i1, -jnp.inf, logits)
        m2 = jnp.max(rest, axis=-1, keepdims=True)
        i2 = jnp.min(jnp.where(rest == m2, lane, LANES), axis=-1, keepdims=True)
        e2 = jnp.exp(m2 - m1)
        g1 = 1.0 / (1.0 + e2)
        g2 = e2 / (1.0 + e2)
        comb_ref[...] = (jnp.where(lane == 0, i1.astype(F32), 0.0) + jnp.where(lane == 1, i2.astype(F32), 0.0)
                         + jnp.where(lane == TOP_K, g1, 0.0) + jnp.where(lane == TOP_K + 1, g2, 0.0))


def _ln_call(x, sub, mod4, ln_g, ln_b, *, layer, gate_row, alpha, n_prompt, t_dec,
             next_layer=None, next_row=None, h_dtype=BF16, w_router=None, n_experts=0, route=None,
             row0=0, n_rows=None):
    nt, d = x.shape
    tm = _tile(min(n_prompt, t_dec), 256)
    n_rows = nt if n_rows is None else n_rows
    tile0 = row0 // tm
    has_next = next_layer is not None
    has_router = w_router is not None
    routed_sub = route is not None
    row_in = pl.BlockSpec((tm, d), lambda i: (tile0 + i, 0))
    row = pl.BlockSpec((tm, d), lambda i: (i, 0))
    vec = pl.BlockSpec((None, 1, d), lambda i: (layer, 0, 0))
    in_specs = [row_in, row_in]
    args = [x, sub]
    if routed_sub:
        in_specs += [pl.BlockSpec((tm, d), lambda i: (nt // tm + tile0 + i, 0)),
                     pl.BlockSpec((tm, LANES), lambda i: (tile0 + i, 0))]
        args += [sub, route]
    in_specs += [_mod_spec(layer, tm, n_prompt, t_dec, d, tile0), vec, vec]
    args += [mod4, ln_g.reshape(-1, 1, d), ln_b.reshape(-1, 1, d)]
    out_shape = [jax.ShapeDtypeStruct((n_rows, d), F32)]
    out_specs = [row]
    if has_next:
        in_specs.append(_mod_spec(next_layer, tm, n_prompt, t_dec, d, tile0))
        args.append(mod4)
        h_cols = d // 2 if h_dtype == jnp.uint32 else d
        out_shape.append(jax.ShapeDtypeStruct((n_rows, h_cols), h_dtype))
        out_specs.append(pl.BlockSpec((tm, h_cols), lambda i: (i, 0)))
    if has_router:
        in_specs.append(pl.BlockSpec((d, LANES), lambda i: (0, 0)))
        args.append(w_router)
        out_shape.append(jax.ShapeDtypeStruct((n_rows, LANES), F32))
        out_specs.append(pl.BlockSpec((tm, LANES), lambda i: (i, 0)))
    body = functools.partial(_ln_body, alpha=alpha, gate_row=gate_row, next_row=next_row, has_next=has_next,
                             has_router=has_router, routed_sub=routed_sub, n_experts=n_experts)
    return pl.pallas_call(
        body, out_shape=out_shape, grid=(n_rows // tm,), in_specs=in_specs, out_specs=out_specs,
        compiler_params=_params("parallel"), name="ln_mod",
    )(*args)


def _rms_rows(x, g):
    return x * lax.rsqrt(jnp.mean(x * x, axis=-1, keepdims=True) + EPS) * g


def _rope_lanes(x, c, s1, s2, seg):
    return x * c + pltpu.roll(x, LANES - seg, 1) * s1 + pltpu.roll(x, seg, 1) * s2


def _epi_none(acc, o_refs):
    o_refs[0][...] = acc.astype(o_refs[0].dtype)


def _epi_rms(acc, g_ref, o_refs):
    o_refs[0][...] = _rms_rows(acc, g_ref[...]).astype(o_refs[0].dtype)


def _epi_mla_q(acc, c_ref, s1_ref, s2_ref, o_refs, *, seg, qscale):
    n_heads = acc.shape[1] // (2 * LANES)
    acc = acc * qscale
    c, s1, s2 = c_ref[...], s1_ref[...], s2_ref[...]
    for h in range(n_heads):
        lo = h * 2 * LANES
        o_refs[0][:, lo:lo + LANES] = acc[:, lo:lo + LANES].astype(o_refs[0].dtype)
        o_refs[0][:, lo + LANES:lo + 2 * LANES] = _rope_lanes(
            acc[:, lo + LANES:lo + 2 * LANES], c, s1, s2, seg).astype(o_refs[0].dtype)


def _epi_mla_kv(acc, g_ref, c_ref, s1_ref, s2_ref, o_refs, *, kv_lora, seg):
    ckv = _rms_rows(acc[:, :kv_lora], g_ref[...])
    kr = acc[:, kv_lora:kv_lora + LANES]
    o_refs[0][:, :kv_lora] = ckv
    o_refs[0][:, kv_lora:kv_lora + LANES] = kr
    o_refs[1][...] = ckv.astype(o_refs[1].dtype)
    o_refs[2][...] = _rope_lanes(kr, c_ref[...], s1_ref[...], s2_ref[...], seg).astype(o_refs[2].dtype)


def _mm_body(*refs, nk, n_extra, n_out, epi):
    x_ref, w_ref = refs[0], refs[1]
    extra = refs[2:2 + n_extra]
    o_refs = refs[2 + n_extra:2 + n_extra + n_out]
    part = jnp.dot(x_ref[...], w_ref[...].astype(BF16), preferred_element_type=F32)
    if nk == 1:
        epi(part, *extra, o_refs)
    else:
        acc_ref = refs[2 + n_extra + n_out]
        k = pl.program_id(2)

        @pl.when(k == 0)
        def _():
            acc_ref[...] = part

        @pl.when(k > 0)
        def _():
            acc_ref[...] += part

        @pl.when(k == nk - 1)
        def _():
            epi(acc_ref[...], *extra, o_refs)


def _matmul(x, w, *, w_map, n, tm, tn, tk=None, out=None, epi=_epi_none, extra=(), extra_specs=(), name="mm",
            x_buffers=None):
    m, kdim = x.shape
    tk = tk or kdim
    nk = kdim // tk
    x_mode = {} if x_buffers is None else dict(pipeline_mode=pl.Buffered(x_buffers))
    if out is None:
        out = [(tn, F32, n)]
    w_block = (None,) * (w.ndim - 2) + (tk, tn)
    out_shape = [jax.ShapeDtypeStruct((m, width), dt) for (_, dt, width) in out]
    out_specs = [pl.BlockSpec((tm, bt), lambda i, j, k: (i, j)) for (bt, _, _) in out]
    body = functools.partial(_mm_body, nk=nk, n_extra=len(extra), n_out=len(out), epi=epi)
    res = pl.pallas_call(
        body,
        out_shape=out_shape,
        grid=(m // tm, n // tn, nk),
        in_specs=[pl.BlockSpec((tm, tk), lambda i, j, k: (i, k), **x_mode),
                  pl.BlockSpec(w_block, lambda i, j, k: w_map(j, k))] + list(extra_specs),
        out_specs=out_specs,
        scratch_shapes=[pltpu.VMEM((tm, tn), F32)] if nk > 1 else [],
        compiler_params=_params("parallel", "parallel", "arbitrary"),
        name=name,
    )(x, w, *extra)
    return res


def _swiglu_tile(x_ref, wg_ref, wu_ref, o_ref):
    x = x_ref[...]
    if x.dtype == jnp.uint32:
        x = _unpack_bf16_halves(x)
    g = jnp.dot(x, wg_ref[...].astype(BF16), preferred_element_type=F32)
    u = jnp.dot(x, wu_ref[...].astype(BF16), preferred_element_type=F32)
    o_ref[...] = (_silu(g) * u).astype(o_ref.dtype)


def _swiglu_call(x, wg, wu, *, w_map, n, tm, tn, x_buffers=None):
    m, kdim = x.shape
    w_block = (None,) * (wg.ndim - 2) + (kdim, tn)
    w_spec = pl.BlockSpec(w_block, lambda i, j: w_map(j))
    x_mode = {} if x_buffers is None else dict(pipeline_mode=pl.Buffered(x_buffers))
    return pl.pallas_call(
        _swiglu_tile,
        out_shape=jax.ShapeDtypeStruct((m, n), BF16),
        grid=(m // tm, n // tn),
        in_specs=[pl.BlockSpec((tm, kdim), lambda i, j: (i, 0), **x_mode), w_spec, w_spec],
        out_specs=pl.BlockSpec((tm, tn), lambda i, j: (i, j)),
        compiler_params=_params("parallel", "parallel"),
        name="swiglu",
    )(x, wg, wu)


GATHER_UNROLL = 8


def _gather_body(idx_ref, src_ref, o_ref, sem, *, tm):
    base = pl.program_id(0) * tm

    def issue(c, carry):
        for u in range(GATHER_UNROLL):
            t = c * GATHER_UNROLL + u
            pltpu.make_async_copy(src_ref.at[pl.ds(idx_ref[base + t], 1)], o_ref.at[pl.ds(t, 1)], sem).start()
        return carry

    lax.fori_loop(0, tm // GATHER_UNROLL, issue, 0)
    pltpu.make_async_copy(src_ref.at[pl.ds(0, tm)], o_ref, sem).wait()


def _gather_rows(src, idx, *, tm):
    r = idx.shape[0]
    d = src.shape[1]
    assert src.dtype.itemsize == 4 and tm % GATHER_UNROLL == 0
    return pl.pallas_call(
        functools.partial(_gather_body, tm=tm),
        out_shape=jax.ShapeDtypeStruct((r, d), src.dtype),
        grid_spec=pltpu.PrefetchScalarGridSpec(
            num_scalar_prefetch=1,
            grid=(r // tm,),
            in_specs=[pl.BlockSpec(memory_space=pl.ANY)],
            out_specs=pl.BlockSpec((tm, d), lambda i, idx_ref: (i, 0)),
            scratch_shapes=[pltpu.SemaphoreType.DMA(())],
        ),
        compiler_params=_params("arbitrary"),
        name="gather_rows",
    )(idx, src)


def _routed_swiglu_body(te_ref, nu_ref, x_ref, wg_ref, wu_ref, o_ref):
    r = pl.program_id(1)

    @pl.when(r < nu_ref[0])
    def _():
        _swiglu_tile(x_ref, wg_ref, wu_ref, o_ref)

    @pl.when(r >= nu_ref[0])
    def _():
        o_ref[...] = jnp.zeros_like(o_ref)


def _routed_down_body(te_ref, nu_ref, x_ref, w_ref, o_ref):
    r = pl.program_id(1)

    @pl.when(r < nu_ref[0])
    def _():
        o_ref[...] = jnp.dot(x_ref[...], w_ref[...].astype(BF16), preferred_element_type=F32)

    @pl.when(r >= nu_ref[0])
    def _():
        o_ref[...] = jnp.zeros_like(o_ref)


def _routed_call(body, x, ws, tile_expert, n_used, *, layer, n, tm, tn, out_dtype, name):
    rows, x_cols = x.shape
    kdim = ws[0].shape[-2]
    used_row = lambda r, nu: jnp.minimum(r, nu[0] - 1)
    w_spec = pl.BlockSpec((None, None, kdim, tn), lambda j, r, te, nu: (layer, te[r], 0, j))
    return pl.pallas_call(
        body,
        out_shape=jax.ShapeDtypeStruct((rows, n), out_dtype),
        grid_spec=pltpu.PrefetchScalarGridSpec(
            num_scalar_prefetch=2,
            grid=(n // tn, rows // tm),
            in_specs=[pl.BlockSpec((tm, x_cols), lambda j, r, te, nu: (used_row(r, nu), 0))] + [w_spec] * len(ws),
            out_specs=pl.BlockSpec((tm, tn), lambda j, r, te, nu: (r, j)),
        ),
        compiler_params=_params("parallel", "arbitrary"),
        name=name,
    )(tile_expert, n_used, x, *ws)


def _route_plan(route, n_experts, tm):
    nt = route.shape[0]
    picks = route[:, :TOP_K].astype(jnp.int32)
    flat = picks.T.reshape(-1)
    onehot = (flat[:, None] == jnp.arange(n_experts, dtype=jnp.int32)[None, :]).astype(jnp.int32)
    csum = jnp.cumsum(onehot, axis=0)
    rank = jnp.take_along_axis(csum, flat[:, None], axis=1)[:, 0] - 1
    counts = csum[-1]
    padded = (counts + tm - 1) // tm * tm
    ends = jnp.cumsum(padded)
    pos = (ends - padded)[flat] + rank
    n_rows = (TOP_K * nt // tm + n_experts) * tm
    tok = jnp.tile(jnp.arange(nt, dtype=jnp.int32), TOP_K)
    idx = jnp.zeros((n_rows,), jnp.int32).at[pos].set(tok)
    n_used = (ends[-1] // tm).astype(jnp.int32)
    tile_start = jnp.minimum(jnp.arange(n_rows // tm, dtype=jnp.int32), n_used - 1) * tm
    tile_expert = jnp.sum((tile_start[:, None] >= ends[None, :]).astype(jnp.int32), axis=1)
    return idx, pos.reshape(TOP_K, nt), tile_expert, n_used.reshape(1)


def _qkv_post_body(qkv_ref, qn_ref, kn_ref, c_ref, s1_ref, s2_ref,
                   qa_ref, ka_ref, va_ref, qb_ref, kb_ref, vb_ref, st_ref, *, ha, ka, hb, kb, seg, qscale):
    c, s1, s2 = c_ref[...], s1_ref[...], s2_ref[...]
    qn, kn = qn_ref[...], kn_ref[...]
    col = 0
    st = 0

    def head(idx):
        return qkv_ref[:, idx * LANES:(idx + 1) * LANES]

    for h in range(ha):
        q = _rms_rows(head(col + h), qn)
        qa_ref[:, h * LANES:(h + 1) * LANES] = (_rope_lanes(q, c, s1, s2, seg) * qscale).astype(qa_ref.dtype)
    col += ha
    for h in range(ka):
        k = _rms_rows(head(col + h), kn)
        st_ref[:, (st + h) * LANES:(st + h + 1) * LANES] = k
        ka_ref[:, h * LANES:(h + 1) * LANES] = _rope_lanes(k, c, s1, s2, seg).astype(ka_ref.dtype)
    col += ka
    st += ka
    for h in range(ka):
        v = head(col + h)
        st_ref[:, (st + h) * LANES:(st + h + 1) * LANES] = v
        va_ref[:, h * LANES:(h + 1) * LANES] = v.astype(va_ref.dtype)
    col += ka
    st += ka
    for h in range(hb):
        qb_ref[:, h * LANES:(h + 1) * LANES] = (_rope_lanes(head(col + h), c, s1, s2, seg) * qscale).astype(qb_ref.dtype)
    col += hb
    for h in range(kb):
        k = head(col + h)
        st_ref[:, (st + h) * LANES:(st + h + 1) * LANES] = k
        kb_ref[:, h * LANES:(h + 1) * LANES] = _rope_lanes(k, c, s1, s2, seg).astype(kb_ref.dtype)
    col += kb
    st += kb
    for h in range(kb):
        v = head(col + h)
        st_ref[:, (st + h) * LANES:(st + h + 1) * LANES] = v
        vb_ref[:, h * LANES:(h + 1) * LANES] = v.astype(vb_ref.dtype)


def _qkv_post_call(qkv, qn_g, kn_g, layer, tabs, *, ha, ka, hb, kb, seg):
    nt, width = qkv.shape
    tm = _tile(nt, 256)
    row = lambda w: pl.BlockSpec((tm, w), lambda i: (i, 0))
    gain = pl.BlockSpec((None, 1, LANES), lambda i: (layer, 0, 0))
    widths = [ha * LANES, ka * LANES, ka * LANES, hb * LANES, kb * LANES, kb * LANES]
    st_w = 2 * (ka + kb) * LANES
    body = functools.partial(_qkv_post_body, ha=ha, ka=ka, hb=hb, kb=kb, seg=seg, qscale=LANES ** -0.5 * LOG2E)
    return pl.pallas_call(
        body,
        out_shape=[jax.ShapeDtypeStruct((nt, w), BF16) for w in widths] + [jax.ShapeDtypeStruct((nt, st_w), F32)],
        grid=(nt // tm,),
        in_specs=[row(width), gain, gain, row(LANES), row(LANES), row(LANES)],
        out_specs=[row(w) for w in widths] + [row(st_w)],
        compiler_params=_params("parallel"),
        name="qkv_post",
    )(qkv, qn_g.reshape(-1, 1, LANES), kn_g.reshape(-1, 1, LANES), *tabs)


def _attn_body(*refs, hps, k_lanes, v_lane, **kw):
    n_kparts = len(k_lanes)
    it = iter(refs)
    q_ref = next(it)
    k_refs = [next(it) for _ in range(n_kparts)]
    v_ref = next(it)
    kc_refs = [next(it) for _ in range(n_kparts)] if kw["has_ctx"] else None
    vc_ref = next(it) if kw["has_ctx"] else None
    sink_ref = next(it) if kw["has_sink"] else None
    o_ref = refs[-1]
    for hh in range(hps):
        lanes = lambda ref, off: ref.at[:, off:off + LANES]
        _attn_head(q_ref, [lanes(r, f(hh)) for r, f in zip(k_refs, k_lanes)], lanes(v_ref, v_lane(hh)),
                   [lanes(r, f(hh)) for r, f in zip(kc_refs, k_lanes)] if kc_refs else None,
                   lanes(vc_ref, v_lane(hh)) if vc_ref is not None else None,
                   sink_ref, o_ref, hh=hh, hps=hps, **kw)


def _attn_head(q_ref, k_refs, v_ref, kc_refs, vc_ref, sink_ref, o_ref, *, hh, hps, g, dq, dv, tq, tk, t,
               window, has_ctx, has_sink):
    h = pl.program_id(1) * hps + hh
    qi = pl.program_id(2)
    rows = g * tq
    q0 = hh * g * dq
    q = (jnp.concatenate([q_ref[:, q0 + i * dq:q0 + (i + 1) * dq] for i in range(g)], axis=0) if g > 1
         else q_ref[:, q0:q0 + dq])

    def scores(parts):
        k = parts[0] if len(parts) == 1 else jnp.concatenate(parts, axis=1)
        return lax.dot_general(q, k.astype(BF16), (((1,), (1,)), ((), ())), preferred_element_type=F32)

    def update(carry, s, v):
        m, acc = carry
        m_new = jnp.maximum(m, jnp.max(s, axis=1, keepdims=True))
        a = jnp.exp2(m - m_new)
        p = jnp.exp2(s - m_new)
        v1 = jnp.concatenate([v.astype(BF16), jnp.ones((v.shape[0], LANES), BF16)], axis=1)
        acc = a * acc + jnp.dot(p.astype(BF16), v1, preferred_element_type=F32)
        return m_new, acc

    carry = (jnp.full((rows, 1), NEG_INF, F32), jnp.zeros((rows, dv + LANES), F32))
    if window is None:
        for c in range(t // tk):
            s = scores([r[c * tk:(c + 1) * tk, :] for r in k_refs])
            carry = update(carry, s, v_ref[c * tk:(c + 1) * tk, :])
    else:
        first = (qi * tq - window) // tk
        for r_ in range((tq + 2 * window) // tk):
            c = first + r_
            start = pl.multiple_of(jnp.clip(c, 0, t // tk - 1) * tk, tk)
            s = scores([r[pl.ds(start, tk), :] for r in k_refs])
            qpos = qi * tq + lax.rem(lax.broadcasted_iota(jnp.int32, s.shape, 0), tq)
            kpos = c * tk + lax.broadcasted_iota(jnp.int32, s.shape, 1)
            ok = (jnp.abs(qpos - kpos) <= window) & (kpos >= 0) & (kpos < t)
            carry = update(carry, jnp.where(ok, s, NEG_INF), v_ref[pl.ds(start, tk), :])
    if has_ctx:
        carry = update(carry, scores([r[...] for r in kc_refs]), vc_ref[...])
    m, acc = carry
    l = acc[:, dv:dv + 1]
    acc = acc[:, :dv]
    if has_sink:
        sk = jnp.concatenate([jnp.full((tq, 1), sink_ref[0, h * g + i] * LOG2E, F32) for i in range(g)], axis=0)
        m_new = jnp.maximum(m, sk)
        a = jnp.exp2(m - m_new)
        l = a * l + jnp.exp2(sk - m_new)
        acc = a * acc
    out = acc / l
    o0 = hh * g * dv
    for i in range(g):
        o_ref[:, o0 + i * dv:o0 + (i + 1) * dv] = out[i * tq:(i + 1) * tq].astype(o_ref.dtype)


def _out_buffer(dead, shape):
    if dead.shape == shape and dead.dtype == BF16:
        return dead
    return jnp.zeros(shape, BF16)


def _attn_call(q, kparts, v, *, nb, t, row0, n_kv, g, dq, dv, tq, tk, hps, out_buf, out_col0,
               window=None, ctx=None, sink=None, name="attn"):
    assert row0 % t == 0 and t % tq == 0 and t % tk == 0 and out_col0 % (hps * g * dv) == 0 and n_kv % hps == 0
    qt = t // tq
    col_blk0 = out_col0 // (hps * g * dv)
    in_specs = [pl.BlockSpec((tq, hps * g * dq), lambda b, h, i: (row0 // tq + b * qt + i, h))]
    args = [q]

    def kv_spec(stride):
        if stride == 0:
            return pl.BlockSpec((t, LANES), lambda b, h, i: (row0 // t + b, 0))
        return pl.BlockSpec((t, hps * stride), lambda b, h, i: (row0 // t + b, h))

    lane_fn = lambda stride, off: (lambda hh: hh * stride + off)
    for arr, stride, off in kparts:
        in_specs.append(kv_spec(stride))
        args.append(arr)
    in_specs.append(kv_spec(v[1]))
    args.append(v[0])
    if ctx is not None:
        args += list(ctx[0])
        in_specs += list(ctx[1])
    if sink is not None:
        args.append(sink)
        in_specs.append(pl.BlockSpec(memory_space=pltpu.SMEM))
    args.append(out_buf)
    in_specs.append(pl.BlockSpec(memory_space=pl.ANY))
    body = functools.partial(_attn_body, hps=hps, k_lanes=[lane_fn(s_, o_) for _, s_, o_ in kparts],
                             v_lane=lane_fn(v[1], v[2]), g=g, dq=dq, dv=dv, tq=tq, tk=tk, t=t,
                             window=window, has_ctx=ctx is not None, has_sink=sink is not None)
    return pl.pallas_call(
        body,
        out_shape=jax.ShapeDtypeStruct(out_buf.shape, out_buf.dtype),
        grid=(nb, n_kv // hps, qt),
        in_specs=in_specs,
        out_specs=pl.BlockSpec((tq, hps * g * dv), lambda b, h, i: (row0 // tq + b * qt + i, col_blk0 + h)),
        input_output_aliases={len(args) - 1: 0},
        compiler_params=_params("parallel", "parallel", "arbitrary"),
        name=name,
    )(*args)


def _rope_tables(n_prompt, n_dec, t_dec, dim):
    seg = dim // 4
    d_axis = dim // 2
    row = jnp.repeat(jnp.arange(t_dec // GRID_W, dtype=F32), GRID_W)
    colp = (jnp.arange(t_dec) % GRID_W).astype(F32)
    inv = ROPE_THETA ** (-jnp.arange(0, d_axis, 2, dtype=F32) / d_axis)
    ar, ac = row[:, None] * inv, colp[:, None] * inv
    z = jnp.zeros_like(ar)
    pad = LANES - dim
    c = jnp.concatenate([jnp.cos(ar), jnp.cos(ar), jnp.cos(ac), jnp.cos(ac), jnp.ones((t_dec, pad), F32)], axis=1)
    s1 = jnp.concatenate([-jnp.sin(ar), z, -jnp.sin(ac), z, jnp.zeros((t_dec, pad), F32)], axis=1)
    s2 = jnp.concatenate([z, jnp.sin(ar), z, jnp.sin(ac), jnp.zeros((t_dec, pad), F32)], axis=1)

    def full(tab, fill):
        return jnp.concatenate([jnp.full((n_prompt, LANES), fill, F32)] + [tab] * n_dec, axis=0)

    return full(c, 1.0), full(s1, 0.0), full(s2, 0.0)


def kernel(x_prompt, x_sample, cache_k_a, cache_v_a, cache_k_b, cache_v_b, cache_ckv, cache_krope, c, c_ctx, w_mod, b_mod, ln1_g, ln1_b, ln2_g, ln2_b, w_qkv_ab, qn_a_g, kn_a_g, sink_b, w_o_ab, w_ff_gate, w_ff_up, w_ff_down, w_dq, qn_c_g, w_uq, w_dkv, kvn_c_g, w_ukv, w_o_c, w_router, we_gate, we_up, we_down):
    nb_p, t_p, d = x_prompt.shape
    nb_d, t_d, _ = x_sample.shape
    n_p, n_d = nb_p * t_p, nb_d * t_d
    nt = n_p + n_d
    depth = w_mod.shape[0]
    alpha = (2.0 * depth) ** 0.25
    past = cache_k_a.shape[2]
    ka, hd = cache_k_a.shape[3], cache_k_a.shape[4]
    kb = cache_k_b.shape[3]
    hb = sink_b.shape[1]
    ha = w_o_ab.shape[1] // hd - hb
    q_lora = w_dq.shape[-1]
    kv_lora = cache_ckv.shape[-1]
    rope_dim = cache_krope.shape[-1]
    n_e, ff_e = we_gate.shape[1], we_gate.shape[3]
    d_ff = w_ff_gate.shape[-1]
    mla_h = (w_uq.shape[-1] - w_ukv.shape[-1] + w_o_c.shape[1]) // rope_dim
    nope = w_uq.shape[-1] // mla_h - rope_dim
    v_dim = w_o_c.shape[1] // mla_h
    assert hd == LANES and nope == LANES and v_dim == LANES and rope_dim <= LANES
    assert nt % past == 0 and n_p % t_p == 0 and n_p % t_d == 0

    x = jnp.concatenate([x_prompt.reshape(n_p, d), x_sample.reshape(n_d, d)], axis=0)
    n_cond = 8
    cond = jnp.concatenate([c_ctx[None, :], c, jnp.zeros((n_cond - 1 - nb_d, d), F32)], axis=0)
    mod4 = _mod_call(cond, w_mod, b_mod).reshape(depth, n_cond, 6, d)

    tabs_ab = _rope_tables(n_p, nb_d, t_d, hd)
    tabs_c = _rope_tables(n_p, nb_d, t_d, rope_dim)

    tm = _tile(min(n_p, t_d), 1024)
    tmap = lambda layer: (lambda j, k: (layer, k, j))

    dq_c = 2 * LANES
    w_uq_p = jnp.pad(w_uq.reshape(w_uq.shape[0], q_lora, mla_h, nope + rope_dim),
                     ((0, 0), (0, 0), (0, 0), (0, dq_c - nope - rope_dim))).reshape(w_uq.shape[0], q_lora, mla_h * dq_c)
    dkv_w = kv_lora + LANES
    w_dkv_p = jnp.pad(w_dkv, ((0, 0), (0, 0), (0, dkv_w - w_dkv.shape[-1])))
    w_router_p = jnp.pad(w_router, ((0, 0), (0, 0), (0, LANES - n_e)))

    h = _modulate_call(x, mod4, 0, n_p, t_d)
    states_ab, states_c = [], []
    for layer in range(depth):
        j = layer // 2
        if layer % 2 == 0:
            n_qkv = w_qkv_ab.shape[-1]
            (qkv,) = _matmul(h, w_qkv_ab, w_map=tmap(j), n=n_qkv, tm=tm, tn=_tile(n_qkv, 512), name="qkv")
            qa, k_a, v_a, qb, k_b, v_b, st = _qkv_post_call(qkv, qn_a_g, kn_a_g, j, tabs_ab,
                                                            ha=ha, ka=ka, hb=hb, kb=kb, seg=hd // 4)
            states_ab.append(st[:n_p])
            ck = lambda arr: arr.reshape(nb_d, arr.shape[1], past, -1)
            ctx_spec = pl.BlockSpec((None, None, past, LANES), lambda b, hh, i: (b, j, 0, hh))
            common = dict(dq=hd, dv=hd)
            per_head = lambda arr: (arr, LANES, 0)
            o = _out_buffer(h, (nt, (ha + hb) * hd))
            o = _attn_call(qa, [per_head(k_a)], per_head(v_a), nb=nb_p, t=t_p, row0=0, n_kv=ka, g=ha // ka, hps=1,
                           out_buf=o, out_col0=0, tq=_tile(t_p, 256), tk=_tile(t_p, 512), name="attn_a_p", **common)
            o = _attn_call(qb, [per_head(k_b)], per_head(v_b), nb=nb_p, t=t_p, row0=0, n_kv=kb, g=hb // kb, hps=kb,
                           out_buf=o, out_col0=ha * hd, tq=_tile(t_p, 256), tk=_tile(t_p, 512),
                           sink=sink_b[j:j + 1], name="attn_b_p", **common)
            o = _attn_call(qa, [per_head(k_a)], per_head(v_a), nb=nb_d, t=t_d, row0=n_p, n_kv=ka, g=ha // ka, hps=1,
                           out_buf=o, out_col0=0, tq=_tile(t_d, 256), tk=_tile(t_d, 1024),
                           ctx=([ck(cache_k_a), ck(cache_v_a)], [ctx_spec, ctx_spec]), name="attn_a_d", **common)
            o = _attn_call(qb, [per_head(k_b)], per_head(v_b), nb=nb_d, t=t_d, row0=n_p, n_kv=kb, g=hb // kb, hps=1,
                           out_buf=o, out_col0=ha * hd, tq=_tile(t_d, 256), tk=WINDOW, window=WINDOW,
                           ctx=([ck(cache_k_b), ck(cache_v_b)], [ctx_spec, ctx_spec]),
                           sink=sink_b[j:j + 1], name="attn_b_d", **common)
            (sub,) = _matmul(o, w_o_ab, w_map=tmap(j), n=d, tm=tm, tn=_tile(d, 512), out=[(_tile(d, 512), BF16, d)],
                             name="wo_ab")
        else:
            (dqn,) = _matmul(h, w_dq, w_map=tmap(j), n=q_lora, tm=tm, tn=q_lora, tk=_tile(d, 2048),
                             out=[(q_lora, BF16, q_lora)], epi=_epi_rms,
                             extra=[qn_c_g.reshape(-1, 1, q_lora)],
                             extra_specs=[pl.BlockSpec((None, 1, q_lora), lambda i, jj, k: (j, 0, 0))], name="dq")
            tab_specs = [pl.BlockSpec((tm, LANES), lambda i, jj, k: (i, 0))] * 3
            nq = mla_h * dq_c
            (q_c,) = _matmul(dqn, w_uq_p, w_map=tmap(j), n=nq, tm=tm, tn=_tile(nq, 2048),
                             out=[(_tile(nq, 2048), BF16, nq)],
                             epi=functools.partial(_epi_mla_q, seg=rope_dim // 4, qscale=(nope + rope_dim) ** -0.5 * LOG2E),
                             extra=list(tabs_c), extra_specs=tab_specs, name="uq")
            st_c, ckv_b, kr_b = _matmul(
                h, w_dkv_p, w_map=tmap(j), n=dkv_w, tm=tm, tn=dkv_w, tk=_tile(d, 2048),
                out=[(dkv_w, F32, dkv_w), (kv_lora, BF16, kv_lora), (LANES, BF16, LANES)],
                epi=functools.partial(_epi_mla_kv, kv_lora=kv_lora, seg=rope_dim // 4),
                extra=[kvn_c_g.reshape(-1, 1, kv_lora)] + list(tabs_c),
                extra_specs=[pl.BlockSpec((None, 1, kv_lora), lambda i, jj, k: (j, 0, 0))] + tab_specs, name="dkv")
            states_c.append(st_c[:n_p])
            c_all = jnp.concatenate([ckv_b, cache_ckv[:, j].reshape(nb_d * past, kv_lora).astype(BF16)], axis=0)
            kr_all = jnp.concatenate(
                [kr_b, jnp.pad(cache_krope[:, j].reshape(nb_d * past, rope_dim), ((0, 0), (0, LANES - rope_dim))).astype(BF16)],
                axis=0)
            n_kv_cols = w_ukv.shape[-1]
            (kv,) = _matmul(c_all, w_ukv, w_map=tmap(j), n=n_kv_cols, tm=_tile(c_all.shape[0], 1024),
                            tn=_tile(n_kv_cols, 4096), out=[(_tile(n_kv_cols, 4096), BF16, n_kv_cols)], name="ukv")
            kv_w = 2 * LANES
            k_parts, v_part = [(kv, kv_w, 0), (kr_all, 0, 0)], (kv, kv_w, LANES)
            common = dict(dq=dq_c, dv=v_dim, n_kv=mla_h, g=1)
            o = _out_buffer(h, (nt, mla_h * v_dim))
            o = _attn_call(q_c, k_parts, v_part, nb=nb_p, t=t_p, row0=0, hps=_tile(mla_h, 8), out_buf=o, out_col0=0,
                           tq=_tile(t_p, 512), tk=_tile(t_p, 512), name="attn_c_p", **common)
            ctx_row = lambda b: nt // past + b
            ctx_specs = [pl.BlockSpec((past, kv_w), lambda b, hh, i: (ctx_row(b), hh)),
                         pl.BlockSpec((past, LANES), lambda b, hh, i: (ctx_row(b), 0)),
                         pl.BlockSpec((past, kv_w), lambda b, hh, i: (ctx_row(b), hh))]
            o = _attn_call(q_c, k_parts, v_part, nb=nb_d, t=t_d, row0=n_p, hps=1, out_buf=o, out_col0=0,
                           tq=_tile(t_d, 1024), tk=_tile(t_d, 1024),
                           ctx=([kv, kr_all, kv], ctx_specs), name="attn_c_d", **common)
            (sub,) = _matmul(o, w_o_c, w_map=tmap(j), n=d, tm=tm, tn=_tile(d, 512), out=[(_tile(d, 512), BF16, d)],
                             name="wo_c")

        ln_kw = dict(alpha=alpha, n_prompt=n_p, t_dec=t_d)
        if layer % 2 == 0:
            x, h = _ln_call(x, sub, mod4, ln1_g, ln1_b, layer=layer, gate_row=2, next_layer=layer, next_row=3, **ln_kw)
            act = _swiglu_call(h, w_ff_gate, w_ff_up, w_map=lambda jj: (j, 0, jj), n=d_ff, tm=tm, tn=_tile(d_ff, 256))
            (f,) = _matmul(act, w_ff_down, w_map=tmap(j), n=d, tm=tm, tn=_tile(d, 256), x_buffers=1,
                           out=[(_tile(d, 256), BF16, d)], name="ff_down")
        else:
            x, h, route = _ln_call(x, sub, mod4, ln1_g, ln1_b, layer=layer, gate_row=2, next_layer=layer, next_row=3,
                                   h_dtype=jnp.uint32, w_router=w_router_p[j], n_experts=n_e, **ln_kw)
            tm_e = _tile(TOP_K * nt, 512)
            idx, pos, tile_expert, n_used = _route_plan(route, n_e, tm_e)
            xs = _gather_rows(h, idx, tm=tm_e)
            act = _routed_call(_routed_swiglu_body, xs, [we_gate, we_up], tile_expert, n_used, layer=j,
                               n=ff_e, tm=tm_e, tn=_tile(ff_e, 512), out_dtype=BF16, name="moe_swiglu")
            ys = _routed_call(_routed_down_body, act, [we_down], tile_expert, n_used, layer=j,
                              n=d, tm=tm_e, tn=_tile(d, 1024), out_dtype=F32, name="moe_down")
            f = _gather_rows(ys, pos.reshape(-1), tm=tm_e)
            ln_kw = dict(route=route, **ln_kw)
        if layer + 1 < depth:
            x, h = _ln_call(x, f, mod4, ln2_g, ln2_b, layer=layer, gate_row=5, next_layer=layer + 1, next_row=0, **ln_kw)
        else:
            (y_prompt,) = _ln_call(x, f, mod4, ln2_g, ln2_b, layer=layer, gate_row=5, row0=0, n_rows=n_p, **ln_kw)
            (y_sample,) = _ln_call(x, f, mod4, ln2_g, ln2_b, layer=layer, gate_row=5, row0=n_p, n_rows=n_d, **ln_kw)

    y_prompt = y_prompt.reshape(nb_p, t_p, d)
    y_sample = y_sample.reshape(nb_d, t_d, d)
    st_ab = jnp.stack(states_ab, axis=0).reshape(len(states_ab), nb_p, t_p, -1)
    st_ab = jnp.transpose(st_ab, (1, 0, 2, 3))
    wa, wb = ka * hd, kb * hd
    new_k_a = st_ab[..., :wa].reshape(nb_p, -1, t_p, ka, hd)
    new_v_a = st_ab[..., wa:2 * wa].reshape(nb_p, -1, t_p, ka, hd)
    new_k_b = st_ab[..., 2 * wa:2 * wa + wb].reshape(nb_p, -1, t_p, kb, hd)
    new_v_b = st_ab[..., 2 * wa + wb:].reshape(nb_p, -1, t_p, kb, hd)
    st_c = jnp.transpose(jnp.stack(states_c, axis=0).reshape(len(states_c), nb_p, t_p, -1), (1, 0, 2, 3))
    new_ckv = st_c[..., :kv_lora]
    new_krope = st_c[..., kv_lora:kv_lora + rope_dim]
    return (y_prompt, y_sample, new_k_a, new_v_a, new_k_b, new_v_b, new_ckv, new_krope)
```

```python
import functools

import jax
import jax.numpy as jnp
from jax import lax
from jax.experimental import pallas as pl
from jax.experimental.pallas import tpu as pltpu

F32 = jnp.float32
BF16 = jnp.bfloat16

GRID_W = 64
WINDOW = 128
ROPE_THETA = 10000.0
EPS = 1e-6
NEG_INF = -1e30
TOP_K = 2
LANES = 128
LOG2E = 1.4426950408889634
VMEM_LIMIT_BYTES = 56 * 2**20


def _params(*sem):
    return pltpu.CompilerParams(dimension_semantics=sem, vmem_limit_bytes=VMEM_LIMIT_BYTES)


def _tile(n, pref):
    t = min(n, pref)
    while n % t:
        t //= 2
    return t


def _silu(x):
    return x / (1.0 + jnp.exp(-x))


def _pack_bf16_halves(x):
    c = x.shape[1] // 2
    bits = pltpu.bitcast(x.astype(BF16).astype(F32), jnp.uint32)
    return (bits[:, :c] >> 16) | (bits[:, c:] & jnp.uint32(0xFFFF0000))


def _unpack_bf16_halves(w):
    lo = pltpu.bitcast(w << 16, F32).astype(BF16)
    hi = pltpu.bitcast(w & jnp.uint32(0xFFFF0000), F32).astype(BF16)
    return jnp.concatenate([lo, hi], axis=1)


def _mod_body(c_ref, w_ref, b_ref, o_ref):
    s = _silu(c_ref[...]).astype(BF16)
    o_ref[...] = jnp.dot(s, w_ref[...].astype(BF16), preferred_element_type=F32) + b_ref[...]


def _mod_call(cond, w_mod, b_mod):
    depth, d, n = w_mod.shape
    r = cond.shape[0]
    tn = _tile(n, 512)
    return pl.pallas_call(
        _mod_body,
        out_shape=jax.ShapeDtypeStruct((depth, r, n), F32),
        grid=(depth, n // tn),
        in_specs=[
            pl.BlockSpec((r, d), lambda l, j: (0, 0)),
            pl.BlockSpec((None, d, tn), lambda l, j: (l, 0, j)),
            pl.BlockSpec((None, 1, tn), lambda l, j: (l, 0, j)),
        ],
        out_specs=pl.BlockSpec((None, r, tn), lambda l, j: (l, 0, j)),
        compiler_params=_params("parallel", "parallel"),
        name="mod",
    )(cond, w_mod, b_mod.reshape(depth, 1, n))


def _cond_of_tile(i, tm, n_prompt, t_dec):
    np_tiles = n_prompt // tm
    return jnp.where(i < np_tiles, 0, 1 + (i - np_tiles) // (t_dec // tm))


def _mod_spec(layer, tm, n_prompt, t_dec, d, tile0=0):
    return pl.BlockSpec((None, None, 6, d), lambda i: (layer, _cond_of_tile(i + tile0, tm, n_prompt, t_dec), 0, 0))


def _modulate_body(x_ref, mod_ref, h_ref):
    sh = mod_ref[0:1, :]
    sc = mod_ref[1:2, :]
    h_ref[...] = (x_ref[...] * (1.0 + sc) + sh).astype(h_ref.dtype)


def _modulate_call(x, mod4, layer, n_prompt, t_dec):
    nt, d = x.shape
    tm = _tile(min(n_prompt, t_dec), 512)
    return pl.pallas_call(
        _modulate_body,
        out_shape=jax.ShapeDtypeStruct((nt, d), BF16),
        grid=(nt // tm,),
        in_specs=[pl.BlockSpec((tm, d), lambda i: (i, 0)), _mod_spec(layer, tm, n_prompt, t_dec, d)],
        out_specs=pl.BlockSpec((tm, d), lambda i: (i, 0)),
        compiler_params=_params("parallel"),
        name="modulate",
    )(x, mod4)


def _ln_body(*refs, alpha, gate_row, next_row, has_next, has_router, routed_sub, n_experts):
    it = iter(refs)
    x_ref, s_ref = next(it), next(it)
    s2_ref, rt_ref = (next(it), next(it)) if routed_sub else (None, None)
    modc_ref, g_ref, b_ref = next(it), next(it), next(it)
    modn_ref = next(it) if has_next else None
    wr_ref = next(it) if has_router else None
    xo_ref = next(it)
    h_ref = next(it) if has_next else None
    comb_ref = next(it) if has_router else None

    gate = modc_ref[gate_row:gate_row + 1, :]
    sub = s_ref[...]
    if routed_sub:
        sub = rt_ref[:, TOP_K:TOP_K + 1] * sub + rt_ref[:, TOP_K + 1:TOP_K + 2] * s2_ref[...]
    y = alpha * x_ref[...] + gate * sub
    mu = jnp.mean(y, axis=-1, keepdims=True)
    yc = y - mu
    var = jnp.mean(yc * yc, axis=-1, keepdims=True)
    xn = yc * lax.rsqrt(var + EPS) * g_ref[...] + b_ref[...]
    xo_ref[...] = xn
    if has_next:
        sh = modn_ref[next_row:next_row + 1, :]
        sc = modn_ref[next_row + 1:next_row + 2, :]
        h = xn * (1.0 + sc) + sh
        if h_ref.dtype == jnp.uint32:
            h_ref[...] = _pack_bf16_halves(h)
        else:
            h_ref[...] = h.astype(h_ref.dtype)
    if has_router:
        logits = jnp.dot(h, wr_ref[...], preferred_element_type=F32, precision=lax.Precision.HIGHEST)
        lane = lax.broadcasted_iota(jnp.int32, logits.shape, 1)
        logits = jnp.where(lane < n_experts, logits, -jnp.inf)
        m1 = jnp.max(logits, axis=-1, keepdims=True)
        i1 = jnp.min(jnp.where(logits == m1, lane, LANES), axis=-1, keepdims=True)
        rest = jnp.where(lane == i1, -jnp.inf, logits)
        m2 = jnp.max(rest, axis=-1, keepdims=True)
        i2 = jnp.min(jnp.where(rest == m2, lane, LANES), axis=-1, keepdims=True)
        e2 = jnp.exp(m2 - m1)
        g1 = 1.0 / (1.0 + e2)
        g2 = e2 / (1.0 + e2)
        comb_ref[...] = (jnp.where(lane == 0, i1.astype(F32), 0.0) + jnp.where(lane == 1, i2.astype(F32), 0.0)
                         + jnp.where(lane == TOP_K, g1, 0.0) + jnp.where(lane == TOP_K + 1, g2, 0.0))


def _ln_call(x, sub, mod4, ln_g, ln_b, *, layer, gate_row, alpha, n_prompt, t_dec,
             next_layer=None, next_row=None, h_dtype=BF16, w_router=None, n_experts=0, route=None,
             row0=0, n_rows=None):
    nt, d = x.shape
    tm = _tile(min(n_prompt, t_dec), 256)
    n_rows = nt if n_rows is None else n_rows
    tile0 = row0 // tm
    has_next = next_layer is not None
    has_router = w_router is not None
    routed_sub = route is not None
    row_in = pl.BlockSpec((tm, d), lambda i: (tile0 + i, 0))
    row = pl.BlockSpec((tm, d), lambda i: (i, 0))
    vec = pl.BlockSpec((None, 1, d), lambda i: (layer, 0, 0))
    in_specs = [row_in, row_in]
    args = [x, sub]
    if routed_sub:
        in_specs += [pl.BlockSpec((tm, d), lambda i: (nt // tm + tile0 + i, 0)),
                     pl.BlockSpec((tm, LANES), lambda i: (tile0 + i, 0))]
        args += [sub, route]
    in_specs += [_mod_spec(layer, tm, n_prompt, t_dec, d, tile0), vec, vec]
    args += [mod4, ln_g.reshape(-1, 1, d), ln_b.reshape(-1, 1, d)]
    out_shape = [jax.ShapeDtypeStruct((n_rows, d), F32)]
    out_specs = [row]
    if has_next:
        in_specs.append(_mod_spec(next_layer, tm, n_prompt, t_dec, d, tile0))
        args.append(mod4)
        h_cols = d // 2 if h_dtype == jnp.uint32 else d
        out_shape.append(jax.ShapeDtypeStruct((n_rows, h_cols), h_dtype))
        out_specs.append(pl.BlockSpec((tm, h_cols), lambda i: (i, 0)))
    if has_router:
        in_specs.append(pl.BlockSpec((d, LANES), lambda i: (0, 0)))
        args.append(w_router)
        out_shape.append(jax.ShapeDtypeStruct((n_rows, LANES), F32))
        out_specs.append(pl.BlockSpec((tm, LANES), lambda i: (i, 0)))
    body = functools.partial(_ln_body, alpha=alpha, gate_row=gate_row, next_row=next_row, has_next=has_next,
                             has_router=has_router, routed_sub=routed_sub, n_experts=n_experts)
    return pl.pallas_call(
        body, out_shape=out_shape, grid=(n_rows // tm,), in_specs=in_specs, out_specs=out_specs,
        compiler_params=_params("parallel"), name="ln_mod",
    )(*args)


def _rms_rows(x, g):
    return x * lax.rsqrt(jnp.mean(x * x, axis=-1, keepdims=True) + EPS) * g


def _rope_lanes(x, c, s1, s2, seg):
    return x * c + pltpu.roll(x, LANES - seg, 1) * s1 + pltpu.roll(x, seg, 1) * s2


def _epi_none(acc, o_refs):
    o_refs[0][...] = acc.astype(o_refs[0].dtype)


def _epi_rms(acc, g_ref, o_refs):
    o_refs[0][...] = _rms_rows(acc, g_ref[...]).astype(o_refs[0].dtype)


def _epi_mla_q(acc, c_ref, s1_ref, s2_ref, o_refs, *, seg, qscale):
    n_heads = acc.shape[1] // (2 * LANES)
    acc = acc * qscale
    c, s1, s2 = c_ref[...], s1_ref[...], s2_ref[...]
    for h in range(n_heads):
        lo = h * 2 * LANES
        o_refs[0][:, lo:lo + LANES] = acc[:, lo:lo + LANES].astype(o_refs[0].dtype)
        o_refs[0][:, lo + LANES:lo + 2 * LANES] = _rope_lanes(
            acc[:, lo + LANES:lo + 2 * LANES], c, s1, s2, seg).astype(o_refs[0].dtype)


def _epi_mla_kv(acc, g_ref, c_ref, s1_ref, s2_ref, o_refs, *, kv_lora, seg):
    ckv = _rms_rows(acc[:, :kv_lora], g_ref[...])
    kr = acc[:, kv_lora:kv_lora + LANES]
    o_refs[0][:, :kv_lora] = ckv
    o_refs[0][:, kv_lora:kv_lora + LANES] = kr
    o_refs[1][...] = ckv.astype(o_refs[1].dtype)
    o_refs[2][...] = _rope_lanes(kr, c_ref[...], s1_ref[...], s2_ref[...], seg).astype(o_refs[2].dtype)


def _mm_body(*refs, nk, n_extra, n_out, epi):
    x_ref, w_ref = refs[0], refs[1]
    extra = refs[2:2 + n_extra]
    o_refs = refs[2 + n_extra:2 + n_extra + n_out]
    part = jnp.dot(x_ref[...], w_ref[...].astype(BF16), preferred_element_type=F32)
    if nk == 1:
        epi(part, *extra, o_refs)
    else:
        acc_ref = refs[2 + n_extra + n_out]
        k = pl.program_id(2)

        @pl.when(k == 0)
        def _():
            acc_ref[...] = part

        @pl.when(k > 0)
        def _():
            acc_ref[...] += part

        @pl.when(k == nk - 1)
        def _():
            epi(acc_ref[...], *extra, o_refs)


def _matmul(x, w, *, w_map, n, tm, tn, tk=None, out=None, epi=_epi_none, extra=(), extra_specs=(), name="mm",
            x_buffers=None):
    m, kdim = x.shape
    tk = tk or kdim
    nk = kdim // tk
    x_mode = {} if x_buffers is None else dict(pipeline_mode=pl.Buffered(x_buffers))
    if out is None:
        out = [(tn, F32, n)]
    w_block = (None,) * (w.ndim - 2) + (tk, tn)
    out_shape = [jax.ShapeDtypeStruct((m, width), dt) for (_, dt, width) in out]
    out_specs = [pl.BlockSpec((tm, bt), lambda i, j, k: (i, j)) for (bt, _, _) in out]
    body = functools.partial(_mm_body, nk=nk, n_extra=len(extra), n_out=len(out), epi=epi)
    res = pl.pallas_call(
        body,
        out_shape=out_shape,
        grid=(m // tm, n // tn, nk),
        in_specs=[pl.BlockSpec((tm, tk), lambda i, j, k: (i, k), **x_mode),
                  pl.BlockSpec(w_block, lambda i, j, k: w_map(j, k))] + list(extra_specs),
        out_specs=out_specs,
        scratch_shapes=[pltpu.VMEM((tm, tn), F32)] if nk > 1 else [],
        compiler_params=_params("parallel", "parallel", "arbitrary"),
        name=name,
    )(x, w, *extra)
    return res


def _swiglu_tile(x_ref, wg_ref, wu_ref, o_ref):
    x = x_ref[...]
    if x.dtype == jnp.uint32:
        x = _unpack_bf16_halves(x)
    g = jnp.dot(x, wg_ref[...].astype(BF16), preferred_element_type=F32)
    u = jnp.dot(x, wu_ref[...].astype(BF16), preferred_element_type=F32)
    o_ref[...] = (_silu(g) * u).astype(o_ref.dtype)


def _swiglu_call(x, wg, wu, *, w_map, n, tm, tn, x_buffers=None):
    m, kdim = x.shape
    w_block = (None,) * (wg.ndim - 2) + (kdim, tn)
    w_spec = pl.BlockSpec(w_block, lambda i, j: w_map(j))
    x_mode = {} if x_buffers is None else dict(pipeline_mode=pl.Buffered(x_buffers))
    return pl.pallas_call(
        _swiglu_tile,
        out_shape=jax.ShapeDtypeStruct((m, n), BF16),
        grid=(m // tm, n // tn),
        in_specs=[pl.BlockSpec((tm, kdim), lambda i, j: (i, 0), **x_mode), w_spec, w_spec],
        out_specs=pl.BlockSpec((tm, tn), lambda i, j: (i, j)),
        compiler_params=_params("parallel", "parallel"),
        name="swiglu",
    )(x, wg, wu)


GATHER_UNROLL = 16


def _gather_body(idx_ref, src_ref, o_ref, sem, *, tm):
    base = pl.program_id(0) * tm

    def issue(c, carry):
        for u in range(GATHER_UNROLL):
            t = c * GATHER_UNROLL + u
            pltpu.make_async_copy(src_ref.at[pl.ds(idx_ref[base + t], 1)], o_ref.at[pl.ds(t, 1)], sem).start()
        return carry

    lax.fori_loop(0, tm // GATHER_UNROLL, issue, 0)
    pltpu.make_async_copy(src_ref.at[pl.ds(0, tm)], o_ref, sem).wait()


def _gather_rows(src, idx, *, tm):
    r = idx.shape[0]
    d = src.shape[1]
    assert src.dtype.itemsize == 4 and tm % GATHER_UNROLL == 0
    return pl.pallas_call(
        functools.partial(_gather_body, tm=tm),
        out_shape=jax.ShapeDtypeStruct((r, d), src.dtype),
        grid_spec=pltpu.PrefetchScalarGridSpec(
            num_scalar_prefetch=1,
            grid=(r // tm,),
            in_specs=[pl.BlockSpec(memory_space=pl.ANY)],
            out_specs=pl.BlockSpec((tm, d), lambda i, idx_ref: (i, 0)),
            scratch_shapes=[pltpu.SemaphoreType.DMA(())],
        ),
        compiler_params=_params("arbitrary"),
        name="gather_rows",
    )(idx, src)


def _routed_swiglu_body(te_ref, nu_ref, x_ref, wg_ref, wu_ref, o_ref):
    r = pl.program_id(1)

    @pl.when(r < nu_ref[0])
    def _():
        _swiglu_tile(x_ref, wg_ref, wu_ref, o_ref)

    @pl.when(r >= nu_ref[0])
    def _():
        o_ref[...] = jnp.zeros_like(o_ref)


def _routed_down_body(te_ref, nu_ref, x_ref, w_ref, o_ref):
    r = pl.program_id(1)

    @pl.when(r < nu_ref[0])
    def _():
        o_ref[...] = jnp.dot(x_ref[...], w_ref[...].astype(BF16), preferred_element_type=F32)

    @pl.when(r >= nu_ref[0])
    def _():
        o_ref[...] = jnp.zeros_like(o_ref)


def _routed_call(body, x, ws, tile_expert, n_used, *, layer, n, tm, tn, out_dtype, name):
    rows, x_cols = x.shape
    kdim = ws[0].shape[-2]
    used_row = lambda r, nu: jnp.minimum(r, nu[0] - 1)
    w_spec = pl.BlockSpec((None, None, kdim, tn), lambda j, r, te, nu: (layer, te[r], 0, j))
    return pl.pallas_call(
        body,
        out_shape=jax.ShapeDtypeStruct((rows, n), out_dtype),
        grid_spec=pltpu.PrefetchScalarGridSpec(
            num_scalar_prefetch=2,
            grid=(n // tn, rows // tm),
            in_specs=[pl.BlockSpec((tm, x_cols), lambda j, r, te, nu: (used_row(r, nu), 0))] + [w_spec] * len(ws),
            out_specs=pl.BlockSpec((tm, tn), lambda j, r, te, nu: (r, j)),
        ),
        compiler_params=_params("parallel", "arbitrary"),
        name=name,
    )(tile_expert, n_used, x, *ws)


def _route_plan(route, n_experts, tm):
    nt = route.shape[0]
    picks = route[:, :TOP_K].astype(jnp.int32)
    flat = picks.T.reshape(-1)
    onehot = (flat[:, None] == jnp.arange(n_experts, dtype=jnp.int32)[None, :]).astype(jnp.int32)
    csum = jnp.cumsum(onehot, axis=0)
    rank = jnp.take_along_axis(csum, flat[:, None], axis=1)[:, 0] - 1
    counts = csum[-1]
    padded = (counts + tm - 1) // tm * tm
    ends = jnp.cumsum(padded)
    pos = (ends - padded)[flat] + rank
    n_rows = (TOP_K * nt // tm + n_experts) * tm
    tok = jnp.tile(jnp.arange(nt, dtype=jnp.int32), TOP_K)
    idx = (jnp.arange(n_rows, dtype=jnp.int32) % nt).at[pos].set(tok)
    n_used = (ends[-1] // tm).astype(jnp.int32)
    tile_start = jnp.minimum(jnp.arange(n_rows // tm, dtype=jnp.int32), n_used - 1) * tm
    tile_expert = jnp.sum((tile_start[:, None] >= ends[None, :]).astype(jnp.int32), axis=1)
    return idx, pos.reshape(TOP_K, nt), tile_expert, n_used.reshape(1)


def _qkv_post_body(qkv_ref, qn_ref, kn_ref, c_ref, s1_ref, s2_ref,
                   qa_ref, ka_ref, va_ref, qb_ref, kb_ref, vb_ref, st_ref, *, ha, ka, hb, kb, seg, qscale):
    c, s1, s2 = c_ref[...], s1_ref[...], s2_ref[...]
    qn, kn = qn_ref[...], kn_ref[...]
    col = 0
    st = 0

    def head(idx):
        return qkv_ref[:, idx * LANES:(idx + 1) * LANES]

    for h in range(ha):
        q = _rms_rows(head(col + h), qn)
        qa_ref[:, h * LANES:(h + 1) * LANES] = (_rope_lanes(q, c, s1, s2, seg) * qscale).astype(qa_ref.dtype)
    col += ha
    for h in range(ka):
        k = _rms_rows(head(col + h), kn)
        st_ref[:, (st + h) * LANES:(st + h + 1) * LANES] = k
        ka_ref[:, h * LANES:(h + 1) * LANES] = _rope_lanes(k, c, s1, s2, seg).astype(ka_ref.dtype)
    col += ka
    st += ka
    for h in range(ka):
        v = head(col + h)
        st_ref[:, (st + h) * LANES:(st + h + 1) * LANES] = v
        va_ref[:, h * LANES:(h + 1) * LANES] = v.astype(va_ref.dtype)
    col += ka
    st += ka
    for h in range(hb):
        qb_ref[:, h * LANES:(h + 1) * LANES] = (_rope_lanes(head(col + h), c, s1, s2, seg) * qscale).astype(qb_ref.dtype)
    col += hb
    for h in range(kb):
        k = head(col + h)
        st_ref[:, (st + h) * LANES:(st + h + 1) * LANES] = k
        kb_ref[:, h * LANES:(h + 1) * LANES] = _rope_lanes(k, c, s1, s2, seg).astype(kb_ref.dtype)
    col += kb
    st += kb
    for h in range(kb):
        v = head(col + h)
        st_ref[:, (st + h) * LANES:(st + h + 1) * LANES] = v
        vb_ref[:, h * LANES:(h + 1) * LANES] = v.astype(vb_ref.dtype)


def _qkv_post_call(qkv, qn_g, kn_g, layer, tabs, *, ha, ka, hb, kb, seg):
    nt, width = qkv.shape
    tm = _tile(nt, 256)
    row = lambda w: pl.BlockSpec((tm, w), lambda i: (i, 0))
    gain = pl.BlockSpec((None, 1, LANES), lambda i: (layer, 0, 0))
    widths = [ha * LANES, ka * LANES, ka * LANES, hb * LANES, kb * LANES, kb * LANES]
    st_w = 2 * (ka + kb) * LANES
    body = functools.partial(_qkv_post_body, ha=ha, ka=ka, hb=hb, kb=kb, seg=seg, qscale=LANES ** -0.5 * LOG2E)
    return pl.pallas_call(
        body,
        out_shape=[jax.ShapeDtypeStruct((nt, w), BF16) for w in widths] + [jax.ShapeDtypeStruct((nt, st_w), F32)],
        grid=(nt // tm,),
        in_specs=[row(width), gain, gain, row(LANES), row(LANES), row(LANES)],
        out_specs=[row(w) for w in widths] + [row(st_w)],
        compiler_params=_params("parallel"),
        name="qkv_post",
    )(qkv, qn_g.reshape(-1, 1, LANES), kn_g.reshape(-1, 1, LANES), *tabs)


def _attn_body(*refs, hps, k_lanes, v_lane, **kw):
    n_kparts = len(k_lanes)
    it = iter(refs)
    q_ref = next(it)
    k_refs = [next(it) for _ in range(n_kparts)]
    v_ref = next(it)
    kc_refs = [next(it) for _ in range(n_kparts)] if kw["has_ctx"] else None
    vc_ref = next(it) if kw["has_ctx"] else None
    sink_ref = next(it) if kw["has_sink"] else None
    o_ref = refs[-1]
    for hh in range(hps):
        lanes = lambda ref, off: ref.at[:, off:off + LANES]
        _attn_head(q_ref, [lanes(r, f(hh)) for r, f in zip(k_refs, k_lanes)], lanes(v_ref, v_lane(hh)),
                   [lanes(r, f(hh)) for r, f in zip(kc_refs, k_lanes)] if kc_refs else None,
                   lanes(vc_ref, v_lane(hh)) if vc_ref is not None else None,
                   sink_ref, o_ref, hh=hh, hps=hps, **kw)


def _attn_head(q_ref, k_refs, v_ref, kc_refs, vc_ref, sink_ref, o_ref, *, hh, hps, g, dq, dv, tq, tk, t,
               window, has_ctx, has_sink):
    h = pl.program_id(1) * hps + hh
    qi = pl.program_id(2)
    rows = g * tq
    q0 = hh * g * dq
    q = (jnp.concatenate([q_ref[:, q0 + i * dq:q0 + (i + 1) * dq] for i in range(g)], axis=0) if g > 1
         else q_ref[:, q0:q0 + dq])

    def scores(parts):
        k = parts[0] if len(parts) == 1 else jnp.concatenate(parts, axis=1)
        return lax.dot_general(q, k.astype(BF16), (((1,), (1,)), ((), ())), preferred_element_type=F32)

    def update(carry, s, v):
        m, acc = carry
        m_new = jnp.maximum(m, jnp.max(s, axis=1, keepdims=True))
        a = jnp.exp2(m - m_new)
        p = jnp.exp2(s - m_new)
        v1 = jnp.concatenate([v.astype(BF16), jnp.ones((v.shape[0], LANES), BF16)], axis=1)
        acc = a * acc + jnp.dot(p.astype(BF16), v1, preferred_element_type=F32)
        return m_new, acc

    carry = (jnp.full((rows, 1), NEG_INF, F32), jnp.zeros((rows, dv + LANES), F32))
    if window is None:
        for c in range(t // tk):
            s = scores([r[c * tk:(c + 1) * tk, :] for r in k_refs])
            carry = update(carry, s, v_ref[c * tk:(c + 1) * tk, :])
    else:
        first = (qi * tq - window) // tk
        for r_ in range((tq + 2 * window) // tk):
            c = first + r_
            start = pl.multiple_of(jnp.clip(c, 0, t // tk - 1) * tk, tk)
            s = scores([r[pl.ds(start, tk), :] for r in k_refs])
            qpos = qi * tq + lax.rem(lax.broadcasted_iota(jnp.int32, s.shape, 0), tq)
            kpos = c * tk + lax.broadcasted_iota(jnp.int32, s.shape, 1)
            ok = (jnp.abs(qpos - kpos) <= window) & (kpos >= 0) & (kpos < t)
            carry = update(carry, jnp.where(ok, s, NEG_INF), v_ref[pl.ds(start, tk), :])
    if has_ctx:
        carry = update(carry, scores([r[...] for r in kc_refs]), vc_ref[...])
    m, acc = carry
    l = acc[:, dv:dv + 1]
    acc = acc[:, :dv]
    if has_sink:
        sk = jnp.concatenate([jnp.full((tq, 1), sink_ref[0, h * g + i] * LOG2E, F32) for i in range(g)], axis=0)
        m_new = jnp.maximum(m, sk)
        a = jnp.exp2(m - m_new)
        l = a * l + jnp.exp2(sk - m_new)
        acc = a * acc
    out = acc / l
    o0 = hh * g * dv
    for i in range(g):
        o_ref[:, o0 + i * dv:o0 + (i + 1) * dv] = out[i * tq:(i + 1) * tq].astype(o_ref.dtype)


def _out_buffer(dead, shape):
    if dead.shape == shape and dead.dtype == BF16:
        return dead
    return jnp.zeros(shape, BF16)


def _attn_call(q, kparts, v, *, nb, t, row0, n_kv, g, dq, dv, tq, tk, hps, out_buf, out_col0,
               window=None, ctx=None, sink=None, name="attn"):
    assert row0 % t == 0 and t % tq == 0 and t % tk == 0 and out_col0 % (hps * g * dv) == 0 and n_kv % hps == 0
    qt = t // tq
    col_blk0 = out_col0 // (hps * g * dv)
    in_specs = [pl.BlockSpec((tq, hps * g * dq), lambda b, h, i: (row0 // tq + b * qt + i, h))]
    args = [q]

    def kv_spec(stride):
        if stride == 0:
            return pl.BlockSpec((t, LANES), lambda b, h, i: (row0 // t + b, 0))
        return pl.BlockSpec((t, hps * stride), lambda b, h, i: (row0 // t + b, h))

    lane_fn = lambda stride, off: (lambda hh: hh * stride + off)
    for arr, stride, off in kparts:
        in_specs.append(kv_spec(stride))
        args.append(arr)
    in_specs.append(kv_spec(v[1]))
    args.append(v[0])
    if ctx is not None:
        args += list(ctx[0])
        in_specs += list(ctx[1])
    if sink is not None:
        args.append(sink)
        in_specs.append(pl.BlockSpec(memory_space=pltpu.SMEM))
    args.append(out_buf)
    in_specs.append(pl.BlockSpec(memory_space=pl.ANY))
    body = functools.partial(_attn_body, hps=hps, k_lanes=[lane_fn(s_, o_) for _, s_, o_ in kparts],
                             v_lane=lane_fn(v[1], v[2]), g=g, dq=dq, dv=dv, tq=tq, tk=tk, t=t,
                             window=window, has_ctx=ctx is not None, has_sink=sink is not None)
    return pl.pallas_call(
        body,
        out_shape=jax.ShapeDtypeStruct(out_buf.shape, out_buf.dtype),
        grid=(nb, n_kv // hps, qt),
        in_specs=in_specs,
        out_specs=pl.BlockSpec((tq, hps * g * dv), lambda b, h, i: (row0 // tq + b * qt + i, col_blk0 + h)),
        input_output_aliases={len(args) - 1: 0},
        compiler_params=_params("parallel", "parallel", "arbitrary"),
        name=name,
    )(*args)


def _rope_tables(n_prompt, n_dec, t_dec, dim):
    seg = dim // 4
    d_axis = dim // 2
    row = jnp.repeat(jnp.arange(t_dec // GRID_W, dtype=F32), GRID_W)
    colp = (jnp.arange(t_dec) % GRID_W).astype(F32)
    inv = ROPE_THETA ** (-jnp.arange(0, d_axis, 2, dtype=F32) / d_axis)
    ar, ac = row[:, None] * inv, colp[:, None] * inv
    z = jnp.zeros_like(ar)
    pad = LANES - dim
    c = jnp.concatenate([jnp.cos(ar), jnp.cos(ar), jnp.cos(ac), jnp.cos(ac), jnp.ones((t_dec, pad), F32)], axis=1)
    s1 = jnp.concatenate([-jnp.sin(ar), z, -jnp.sin(ac), z, jnp.zeros((t_dec, pad), F32)], axis=1)
    s2 = jnp.concatenate([z, jnp.sin(ar), z, jnp.sin(ac), jnp.zeros((t_dec, pad), F32)], axis=1)

    def full(tab, fill):
        return jnp.concatenate([jnp.full((n_prompt, LANES), fill, F32)] + [tab] * n_dec, axis=0)

    return full(c, 1.0), full(s1, 0.0), full(s2, 0.0)


def kernel(x_prompt, x_sample, cache_k_a, cache_v_a, cache_k_b, cache_v_b, cache_ckv, cache_krope, c, c_ctx, w_mod, b_mod, ln1_g, ln1_b, ln2_g, ln2_b, w_qkv_ab, qn_a_g, kn_a_g, sink_b, w_o_ab, w_ff_gate, w_ff_up, w_ff_down, w_dq, qn_c_g, w_uq, w_dkv, kvn_c_g, w_ukv, w_o_c, w_router, we_gate, we_up, we_down):
    nb_p, t_p, d = x_prompt.shape
    nb_d, t_d, _ = x_sample.shape
    n_p, n_d = nb_p * t_p, nb_d * t_d
    nt = n_p + n_d
    depth = w_mod.shape[0]
    alpha = (2.0 * depth) ** 0.25
    past = cache_k_a.shape[2]
    ka, hd = cache_k_a.shape[3], cache_k_a.shape[4]
    kb = cache_k_b.shape[3]
    hb = sink_b.shape[1]
    ha = w_o_ab.shape[1] // hd - hb
    q_lora = w_dq.shape[-1]
    kv_lora = cache_ckv.shape[-1]
    rope_dim = cache_krope.shape[-1]
    n_e, ff_e = we_gate.shape[1], we_gate.shape[3]
    d_ff = w_ff_gate.shape[-1]
    mla_h = (w_uq.shape[-1] - w_ukv.shape[-1] + w_o_c.shape[1]) // rope_dim
    nope = w_uq.shape[-1] // mla_h - rope_dim
    v_dim = w_o_c.shape[1] // mla_h
    assert hd == LANES and nope == LANES and v_dim == LANES and rope_dim <= LANES
    assert nt % past == 0 and n_p % t_p == 0 and n_p % t_d == 0

    x = jnp.concatenate([x_prompt.reshape(n_p, d), x_sample.reshape(n_d, d)], axis=0)
    n_cond = 8
    cond = jnp.concatenate([c_ctx[None, :], c, jnp.zeros((n_cond - 1 - nb_d, d), F32)], axis=0)
    mod4 = _mod_call(cond, w_mod, b_mod).reshape(depth, n_cond, 6, d)

    tabs_ab = _rope_tables(n_p, nb_d, t_d, hd)
    tabs_c = _rope_tables(n_p, nb_d, t_d, rope_dim)

    tm = _tile(min(n_p, t_d), 1024)
    tmap = lambda layer: (lambda j, k: (layer, k, j))

    dq_c = 2 * LANES
    w_uq_p = jnp.pad(w_uq.reshape(w_uq.shape[0], q_lora, mla_h, nope + rope_dim),
                     ((0, 0), (0, 0), (0, 0), (0, dq_c - nope - rope_dim))).reshape(w_uq.shape[0], q_lora, mla_h * dq_c)
    dkv_w = kv_lora + LANES
    w_dkv_p = jnp.pad(w_dkv, ((0, 0), (0, 0), (0, dkv_w - w_dkv.shape[-1])))
    w_router_p = jnp.pad(w_router, ((0, 0), (0, 0), (0, LANES - n_e)))

    h = _modulate_call(x, mod4, 0, n_p, t_d)
    states_ab, states_c = [], []
    for layer in range(depth):
        j = layer // 2
        if layer % 2 == 0:
            n_qkv = w_qkv_ab.shape[-1]
            (qkv,) = _matmul(h, w_qkv_ab, w_map=tmap(j), n=n_qkv, tm=tm, tn=_tile(n_qkv, 512), name="qkv")
            qa, k_a, v_a, qb, k_b, v_b, st = _qkv_post_call(qkv, qn_a_g, kn_a_g, j, tabs_ab,
                                                            ha=ha, ka=ka, hb=hb, kb=kb, seg=hd // 4)
            states_ab.append(st[:n_p])
            ck = lambda arr: arr.reshape(nb_d, arr.shape[1], past, -1)
            ctx_spec = pl.BlockSpec((None, None, past, LANES), lambda b, hh, i: (b, j, 0, hh))
            common = dict(dq=hd, dv=hd)
            per_head = lambda arr: (arr, LANES, 0)
            o = _out_buffer(h, (nt, (ha + hb) * hd))
            o = _attn_call(qa, [per_head(k_a)], per_head(v_a), nb=nb_p, t=t_p, row0=0, n_kv=ka, g=ha // ka, hps=1,
                           out_buf=o, out_col0=0, tq=_tile(t_p, 256), tk=_tile(t_p, 512), name="attn_a_p", **common)
            o = _attn_call(qb, [per_head(k_b)], per_head(v_b), nb=nb_p, t=t_p, row0=0, n_kv=kb, g=hb // kb, hps=kb,
                           out_buf=o, out_col0=ha * hd, tq=_tile(t_p, 256), tk=_tile(t_p, 512),
                           sink=sink_b[j:j + 1], name="attn_b_p", **common)
            o = _attn_call(qa, [per_head(k_a)], per_head(v_a), nb=nb_d, t=t_d, row0=n_p, n_kv=ka, g=ha // ka, hps=1,
                           out_buf=o, out_col0=0, tq=_tile(t_d, 256), tk=_tile(t_d, 1024),
                           ctx=([ck(cache_k_a), ck(cache_v_a)], [ctx_spec, ctx_spec]), name="attn_a_d", **common)
            o = _attn_call(qb, [per_head(k_b)], per_head(v_b), nb=nb_d, t=t_d, row0=n_p, n_kv=kb, g=hb // kb, hps=1,
                           out_buf=o, out_col0=ha * hd, tq=_tile(t_d, 256), tk=WINDOW, window=WINDOW,
                           ctx=([ck(cache_k_b), ck(cache_v_b)], [ctx_spec, ctx_spec]),
                           sink=sink_b[j:j + 1], name="attn_b_d", **common)
            (sub,) = _matmul(o, w_o_ab, w_map=tmap(j), n=d, tm=tm, tn=_tile(d, 512), out=[(_tile(d, 512), BF16, d)],
                             name="wo_ab")
        else:
            (dqn,) = _matmul(h, w_dq, w_map=tmap(j), n=q_lora, tm=tm, tn=q_lora, tk=_tile(d, 2048),
                             out=[(q_lora, BF16, q_lora)], epi=_epi_rms,
                             extra=[qn_c_g.reshape(-1, 1, q_lora)],
                             extra_specs=[pl.BlockSpec((None, 1, q_lora), lambda i, jj, k: (j, 0, 0))], name="dq")
            tab_specs = [pl.BlockSpec((tm, LANES), lambda i, jj, k: (i, 0))] * 3
            nq = mla_h * dq_c
            (q_c,) = _matmul(dqn, w_uq_p, w_map=tmap(j), n=nq, tm=tm, tn=_tile(nq, 2048),
                             out=[(_tile(nq, 2048), BF16, nq)],
                             epi=functools.partial(_epi_mla_q, seg=rope_dim // 4, qscale=(nope + rope_dim) ** -0.5 * LOG2E),
                             extra=list(tabs_c), extra_specs=tab_specs, name="uq")
            st_c, ckv_b, kr_b = _matmul(
                h, w_dkv_p, w_map=tmap(j), n=dkv_w, tm=tm, tn=dkv_w, tk=_tile(d, 2048),
                out=[(dkv_w, F32, dkv_w), (kv_lora, BF16, kv_lora), (LANES, BF16, LANES)],
                epi=functools.partial(_epi_mla_kv, kv_lora=kv_lora, seg=rope_dim // 4),
                extra=[kvn_c_g.reshape(-1, 1, kv_lora)] + list(tabs_c),
                extra_specs=[pl.BlockSpec((None, 1, kv_lora), lambda i, jj, k: (j, 0, 0))] + tab_specs, name="dkv")
            states_c.append(st_c[:n_p])
            c_all = jnp.concatenate([ckv_b, cache_ckv[:, j].reshape(nb_d * past, kv_lora).astype(BF16)], axis=0)
            kr_all = jnp.concatenate(
                [kr_b, jnp.pad(cache_krope[:, j].reshape(nb_d * past, rope_dim), ((0, 0), (0, LANES - rope_dim))).astype(BF16)],
                axis=0)
            n_kv_cols = w_ukv.shape[-1]
            (kv,) = _matmul(c_all, w_ukv, w_map=tmap(j), n=n_kv_cols, tm=_tile(c_all.shape[0], 1024),
                            tn=_tile(n_kv_cols, 4096), out=[(_tile(n_kv_cols, 4096), BF16, n_kv_cols)], name="ukv")
            kv_w = 2 * LANES
            k_parts, v_part = [(kv, kv_w, 0), (kr_all, 0, 0)], (kv, kv_w, LANES)
            common = dict(dq=dq_c, dv=v_dim, n_kv=mla_h, g=1)
            o = _out_buffer(h, (nt, mla_h * v_dim))
            o = _attn_call(q_c, k_parts, v_part, nb=nb_p, t=t_p, row0=0, hps=_tile(mla_h, 8), out_buf=o, out_col0=0,
                           tq=_tile(t_p, 512), tk=_tile(t_p, 512), name="attn_c_p", **common)
            ctx_row = lambda b: nt // past + b
            ctx_specs = [pl.BlockSpec((past, kv_w), lambda b, hh, i: (ctx_row(b), hh)),
                         pl.BlockSpec((past, LANES), lambda b, hh, i: (ctx_row(b), 0)),
                         pl.BlockSpec((past, kv_w), lambda b, hh, i: (ctx_row(b), hh))]
            o = _attn_call(q_c, k_parts, v_part, nb=nb_d, t=t_d, row0=n_p, hps=1, out_buf=o, out_col0=0,
                           tq=_tile(t_d, 1024), tk=_tile(t_d, 1024),
                           ctx=([kv, kr_all, kv], ctx_specs), name="attn_c_d", **common)
            (sub,) = _matmul(o, w_o_c, w_map=tmap(j), n=d, tm=tm, tn=_tile(d, 512), out=[(_tile(d, 512), BF16, d)],
                             name="wo_c")

        ln_kw = dict(alpha=alpha, n_prompt=n_p, t_dec=t_d)
        if layer % 2 == 0:
            x, h = _ln_call(x, sub, mod4, ln1_g, ln1_b, layer=layer, gate_row=2, next_layer=layer, next_row=3, **ln_kw)
            act = _swiglu_call(h, w_ff_gate, w_ff_up, w_map=lambda jj: (j, 0, jj), n=d_ff, tm=tm, tn=_tile(d_ff, 256))
            (f,) = _matmul(act, w_ff_down, w_map=tmap(j), n=d, tm=tm, tn=_tile(d, 256), x_buffers=1,
                           out=[(_tile(d, 256), BF16, d)], name="ff_down")
        else:
            x, h, route = _ln_call(x, sub, mod4, ln1_g, ln1_b, layer=layer, gate_row=2, next_layer=layer, next_row=3,
                                   h_dtype=jnp.uint32, w_router=w_router_p[j], n_experts=n_e, **ln_kw)
            tm_e = _tile(TOP_K * nt, 512)
            idx, pos, tile_expert, n_used = _route_plan(route, n_e, tm_e)
            xs = _gather_rows(h, idx, tm=tm_e)
            act = _routed_call(_routed_swiglu_body, xs, [we_gate, we_up], tile_expert, n_used, layer=j,
                               n=ff_e, tm=tm_e, tn=_tile(ff_e, 512), out_dtype=BF16, name="moe_swiglu")
            ys = _routed_call(_routed_down_body, act, [we_down], tile_expert, n_used, layer=j,
                              n=d, tm=tm_e, tn=_tile(d, 1024), out_dtype=F32, name="moe_down")
            f = _gather_rows(ys, pos.reshape(-1), tm=tm_e)
            ln_kw = dict(route=route, **ln_kw)
        if layer + 1 < depth:
            x, h = _ln_call(x, f, mod4, ln2_g, ln2_b, layer=layer, gate_row=5, next_layer=layer + 1, next_row=0, **ln_kw)
        else:
            (y_prompt,) = _ln_call(x, f, mod4, ln2_g, ln2_b, layer=layer, gate_row=5, row0=0, n_rows=n_p, **ln_kw)
            (y_sample,) = _ln_call(x, f, mod4, ln2_g, ln2_b, layer=layer, gate_row=5, row0=n_p, n_rows=n_d, **ln_kw)

    y_prompt = y_prompt.reshape(nb_p, t_p, d)
    y_sample = y_sample.reshape(nb_d, t_d, d)
    st_ab = jnp.stack(states_ab, axis=0).reshape(len(states_ab), nb_p, t_p, -1)
    st_ab = jnp.transpose(st_ab, (1, 0, 2, 3))
    wa, wb = ka * hd, kb * hd
    new_k_a = st_ab[..., :wa].reshape(nb_p, -1, t_p, ka, hd)
    new_v_a = st_ab[..., wa:2 * wa].reshape(nb_p, -1, t_p, ka, hd)
    new_k_b = st_ab[..., 2 * wa:2 * wa + wb].reshape(nb_p, -1, t_p, kb, hd)
    new_v_b = st_ab[..., 2 * wa + wb:].reshape(nb_p, -1, t_p, kb, hd)
    st_c = jnp.transpose(jnp.stack(states_c, axis=0).reshape(len(states_c), nb_p, t_p, -1), (1, 0, 2, 3))
    new_ckv = st_c[..., :kv_lora]
    new_krope = st_c[..., kv_lora:kv_lora + rope_dim]
    return (y_prompt, y_sample, new_k_a, new_v_a, new_k_b, new_v_b, new_ckv, new_krope)
```

```python
import functools

import jax
import jax.numpy as jnp
from jax import lax
from jax.experimental import pallas as pl
from jax.experimental.pallas import tpu as pltpu

F32 = jnp.float32
BF16 = jnp.bfloat16

GRID_W = 64
WINDOW = 128
ROPE_THETA = 10000.0
EPS = 1e-6
NEG_INF = -1e30
TOP_K = 2
LANES = 128
LOG2E = 1.4426950408889634
VMEM_LIMIT_BYTES = 56 * 2**20


def _params(*sem):
    return pltpu.CompilerParams(dimension_semantics=sem, vmem_limit_bytes=VMEM_LIMIT_BYTES)


def _tile(n, pref):
    t = min(n, pref)
    while n % t:
        t //= 2
    return t


def _silu(x):
    return x / (1.0 + jnp.exp(-x))


def _pack_bf16_halves(x):
    c = x.shape[1] // 2
    bits = pltpu.bitcast(x.astype(BF16).astype(F32), jnp.uint32)
    return (bits[:, :c] >> 16) | (bits[:, c:] & jnp.uint32(0xFFFF0000))


def _unpack_bf16_halves(w, group=None):
    lo = pltpu.bitcast(w << 16, F32).astype(BF16)
    hi = pltpu.bitcast(w & jnp.uint32(0xFFFF0000), F32).astype(BF16)
    group = w.shape[1] if group is None else group
    parts = []
    for s in range(0, w.shape[1], group):
        parts += [lo[:, s:s + group], hi[:, s:s + group]]
    return jnp.concatenate(parts, axis=1)


def _mod_body(c_ref, w_ref, b_ref, o_ref):
    s = _silu(c_ref[...]).astype(BF16)
    o_ref[...] = jnp.dot(s, w_ref[...].astype(BF16), preferred_element_type=F32) + b_ref[...]


def _mod_call(cond, w_mod, b_mod):
    depth, d, n = w_mod.shape
    r = cond.shape[0]
    tn = _tile(n, 512)
    return pl.pallas_call(
        _mod_body,
        out_shape=jax.ShapeDtypeStruct((depth, r, n), F32),
        grid=(depth, n // tn),
        in_specs=[
            pl.BlockSpec((r, d), lambda l, j: (0, 0)),
            pl.BlockSpec((None, d, tn), lambda l, j: (l, 0, j)),
            pl.BlockSpec((None, 1, tn), lambda l, j: (l, 0, j)),
        ],
        out_specs=pl.BlockSpec((None, r, tn), lambda l, j: (l, 0, j)),
        compiler_params=_params("parallel", "parallel"),
        name="mod",
    )(cond, w_mod, b_mod.reshape(depth, 1, n))


def _cond_of_tile(i, tm, n_prompt, t_dec):
    np_tiles = n_prompt // tm
    return jnp.where(i < np_tiles, 0, 1 + (i - np_tiles) // (t_dec // tm))


def _mod_spec(layer, tm, n_prompt, t_dec, d, tile0=0):
    return pl.BlockSpec((None, None, 6, d), lambda i: (layer, _cond_of_tile(i + tile0, tm, n_prompt, t_dec), 0, 0))


def _modulate_body(x_ref, mod_ref, h_ref):
    sh = mod_ref[0:1, :]
    sc = mod_ref[1:2, :]
    h_ref[...] = (x_ref[...] * (1.0 + sc) + sh).astype(h_ref.dtype)


def _modulate_call(x, mod4, layer, n_prompt, t_dec):
    nt, d = x.shape
    tm = _tile(min(n_prompt, t_dec), 512)
    return pl.pallas_call(
        _modulate_body,
        out_shape=jax.ShapeDtypeStruct((nt, d), BF16),
        grid=(nt // tm,),
        in_specs=[pl.BlockSpec((tm, d), lambda i: (i, 0)), _mod_spec(layer, tm, n_prompt, t_dec, d)],
        out_specs=pl.BlockSpec((tm, d), lambda i: (i, 0)),
        compiler_params=_params("parallel"),
        name="modulate",
    )(x, mod4)


def _ln_body(*refs, alpha, gate_row, next_row, has_next, has_router, routed_sub, sub_group, n_experts):
    it = iter(refs)
    x_ref, s_ref = next(it), next(it)
    s2_ref, rt_ref = (next(it), next(it)) if routed_sub else (None, None)
    modc_ref, g_ref, b_ref = next(it), next(it), next(it)
    modn_ref = next(it) if has_next else None
    wr_ref = next(it) if has_router else None
    xo_ref = next(it)
    h_ref = next(it) if has_next else None
    comb_ref = next(it) if has_router else None

    gate = modc_ref[gate_row:gate_row + 1, :]
    if routed_sub:
        sub = (rt_ref[:, TOP_K:TOP_K + 1] * _unpack_bf16_halves(s_ref[...], sub_group)
               + rt_ref[:, TOP_K + 1:TOP_K + 2] * _unpack_bf16_halves(s2_ref[...], sub_group))
    else:
        sub = s_ref[...]
    y = alpha * x_ref[...] + gate * sub
    mu = jnp.mean(y, axis=-1, keepdims=True)
    yc = y - mu
    var = jnp.mean(yc * yc, axis=-1, keepdims=True)
    xn = yc * lax.rsqrt(var + EPS) * g_ref[...] + b_ref[...]
    xo_ref[...] = xn
    if has_next:
        sh = modn_ref[next_row:next_row + 1, :]
        sc = modn_ref[next_row + 1:next_row + 2, :]
        h = xn * (1.0 + sc) + sh
        if h_ref.dtype == jnp.uint32:
            h_ref[...] = _pack_bf16_halves(h)
        else:
            h_ref[...] = h.astype(h_ref.dtype)
    if has_router:
        logits = jnp.dot(h, wr_ref[...], preferred_element_type=F32, precision=lax.Precision.HIGHEST)
        lane = lax.broadcasted_iota(jnp.int32, logits.shape, 1)
        logits = jnp.where(lane < n_experts, logits, -jnp.inf)
        m1 = jnp.max(logits, axis=-1, keepdims=True)
        i1 = jnp.min(jnp.where(logits == m1, lane, LANES), axis=-1, keepdims=True)
        rest = jnp.where(lane == i1, -jnp.inf, logits)
        m2 = jnp.max(rest, axis=-1, keepdims=True)
        i2 = jnp.min(jnp.where(rest == m2, lane, LANES), axis=-1, keepdims=True)
        e2 = jnp.exp(m2 - m1)
        g1 = 1.0 / (1.0 + e2)
        g2 = e2 / (1.0 + e2)
        comb_ref[...] = (jnp.where(lane == 0, i1.astype(F32), 0.0) + jnp.where(lane == 1, i2.astype(F32), 0.0)
                         + jnp.where(lane == TOP_K, g1, 0.0) + jnp.where(lane == TOP_K + 1, g2, 0.0))


def _ln_call(x, sub, mod4, ln_g, ln_b, *, layer, gate_row, alpha, n_prompt, t_dec,
             next_layer=None, next_row=None, h_dtype=BF16, w_router=None, n_experts=0, route=None, sub_group=None,
             row0=0, n_rows=None):
    nt, d = x.shape
    tm = _tile(min(n_prompt, t_dec), 256)
    n_rows = nt if n_rows is None else n_rows
    tile0 = row0 // tm
    has_next = next_layer is not None
    has_router = w_router is not None
    routed_sub = route is not None
    row_in = pl.BlockSpec((tm, d), lambda i: (tile0 + i, 0))
    row = pl.BlockSpec((tm, d), lambda i: (i, 0))
    vec = pl.BlockSpec((None, 1, d), lambda i: (layer, 0, 0))
    in_specs = [row_in, row_in]
    args = [x, sub]
    if routed_sub:
        in_specs = [row_in, pl.BlockSpec((tm, d // 2), lambda i: (tile0 + i, 0)),
                    pl.BlockSpec((tm, d // 2), lambda i: (nt // tm + tile0 + i, 0)),
                    pl.BlockSpec((tm, LANES), lambda i: (tile0 + i, 0))]
        args += [sub, route]
    in_specs += [_mod_spec(layer, tm, n_prompt, t_dec, d, tile0), vec, vec]
    args += [mod4, ln_g.reshape(-1, 1, d), ln_b.reshape(-1, 1, d)]
    out_shape = [jax.ShapeDtypeStruct((n_rows, d), F32)]
    out_specs = [row]
    if has_next:
        in_specs.append(_mod_spec(next_layer, tm, n_prompt, t_dec, d, tile0))
        args.append(mod4)
        h_cols = d // 2 if h_dtype == jnp.uint32 else d
        out_shape.append(jax.ShapeDtypeStruct((n_rows, h_cols), h_dtype))
        out_specs.append(pl.BlockSpec((tm, h_cols), lambda i: (i, 0)))
    if has_router:
        in_specs.append(pl.BlockSpec((d, LANES), lambda i: (0, 0)))
        args.append(w_router)
        out_shape.append(jax.ShapeDtypeStruct((n_rows, LANES), F32))
        out_specs.append(pl.BlockSpec((tm, LANES), lambda i: (i, 0)))
    body = functools.partial(_ln_body, alpha=alpha, gate_row=gate_row, next_row=next_row, has_next=has_next,
                             has_router=has_router, routed_sub=routed_sub, sub_group=sub_group, n_experts=n_experts)
    return pl.pallas_call(
        body, out_shape=out_shape, grid=(n_rows // tm,), in_specs=in_specs, out_specs=out_specs,
        compiler_params=_params("parallel"), name="ln_mod",
    )(*args)


def _rms_rows(x, g):
    return x * lax.rsqrt(jnp.mean(x * x, axis=-1, keepdims=True) + EPS) * g


def _rope_lanes(x, c, s1, s2, seg):
    return x * c + pltpu.roll(x, LANES - seg, 1) * s1 + pltpu.roll(x, seg, 1) * s2


def _epi_none(acc, o_refs):
    o_refs[0][...] = acc.astype(o_refs[0].dtype)


def _epi_rms(acc, g_ref, o_refs):
    o_refs[0][...] = _rms_rows(acc, g_ref[...]).astype(o_refs[0].dtype)


def _epi_mla_q(acc, c_ref, s1_ref, s2_ref, o_refs, *, seg, qscale):
    n_heads = acc.shape[1] // (2 * LANES)
    acc = acc * qscale
    c, s1, s2 = c_ref[...], s1_ref[...], s2_ref[...]
    for h in range(n_heads):
        lo = h * 2 * LANES
        o_refs[0][:, lo:lo + LANES] = acc[:, lo:lo + LANES].astype(o_refs[0].dtype)
        o_refs[0][:, lo + LANES:lo + 2 * LANES] = _rope_lanes(
            acc[:, lo + LANES:lo + 2 * LANES], c, s1, s2, seg).astype(o_refs[0].dtype)


def _epi_mla_kv(acc, g_ref, c_ref, s1_ref, s2_ref, o_refs, *, kv_lora, seg):
    ckv = _rms_rows(acc[:, :kv_lora], g_ref[...])
    kr = acc[:, kv_lora:kv_lora + LANES]
    o_refs[0][:, :kv_lora] = ckv
    o_refs[0][:, kv_lora:kv_lora + LANES] = kr
    o_refs[1][...] = ckv.astype(o_refs[1].dtype)
    o_refs[2][...] = _rope_lanes(kr, c_ref[...], s1_ref[...], s2_ref[...], seg).astype(o_refs[2].dtype)


def _mm_body(*refs, nk, n_extra, n_out, epi):
    x_ref, w_ref = refs[0], refs[1]
    extra = refs[2:2 + n_extra]
    o_refs = refs[2 + n_extra:2 + n_extra + n_out]
    part = jnp.dot(x_ref[...], w_ref[...].astype(BF16), preferred_element_type=F32)
    if nk == 1:
        epi(part, *extra, o_refs)
    else:
        acc_ref = refs[2 + n_extra + n_out]
        k = pl.program_id(2)

        @pl.when(k == 0)
        def _():
            acc_ref[...] = part

        @pl.when(k > 0)
        def _():
            acc_ref[...] += part

        @pl.when(k == nk - 1)
        def _():
            epi(acc_ref[...], *extra, o_refs)


def _matmul(x, w, *, w_map, n, tm, tn, tk=None, out=None, epi=_epi_none, extra=(), extra_specs=(), name="mm",
            x_buffers=None):
    m, kdim = x.shape
    tk = tk or kdim
    nk = kdim // tk
    x_mode = {} if x_buffers is None else dict(pipeline_mode=pl.Buffered(x_buffers))
    if out is None:
        out = [(tn, F32, n)]
    w_block = (None,) * (w.ndim - 2) + (tk, tn)
    out_shape = [jax.ShapeDtypeStruct((m, width), dt) for (_, dt, width) in out]
    out_specs = [pl.BlockSpec((tm, bt), lambda i, j, k: (i, j)) for (bt, _, _) in out]
    body = functools.partial(_mm_body, nk=nk, n_extra=len(extra), n_out=len(out), epi=epi)
    res = pl.pallas_call(
        body,
        out_shape=out_shape,
        grid=(m // tm, n // tn, nk),
        in_specs=[pl.BlockSpec((tm, tk), lambda i, j, k: (i, k), **x_mode),
                  pl.BlockSpec(w_block, lambda i, j, k: w_map(j, k))] + list(extra_specs),
        out_specs=out_specs,
        scratch_shapes=[pltpu.VMEM((tm, tn), F32)] if nk > 1 else [],
        compiler_params=_params("parallel", "parallel", "arbitrary"),
        name=name,
    )(x, w, *extra)
    return res


def _swiglu_tile(x_ref, wg_ref, wu_ref, o_ref):
    x = x_ref[...]
    if x.dtype == jnp.uint32:
        x = _unpack_bf16_halves(x)
    g = jnp.dot(x, wg_ref[...].astype(BF16), preferred_element_type=F32)
    u = jnp.dot(x, wu_ref[...].astype(BF16), preferred_element_type=F32)
    o_ref[...] = (_silu(g) * u).astype(o_ref.dtype)


def _swiglu_call(x, wg, wu, *, w_map, n, tm, tn, x_buffers=None):
    m, kdim = x.shape
    w_block = (None,) * (wg.ndim - 2) + (kdim, tn)
    w_spec = pl.BlockSpec(w_block, lambda i, j: w_map(j))
    x_mode = {} if x_buffers is None else dict(pipeline_mode=pl.Buffered(x_buffers))
    return pl.pallas_call(
        _swiglu_tile,
        out_shape=jax.ShapeDtypeStruct((m, n), BF16),
        grid=(m // tm, n // tn),
        in_specs=[pl.BlockSpec((tm, kdim), lambda i, j: (i, 0), **x_mode), w_spec, w_spec],
        out_specs=pl.BlockSpec((tm, tn), lambda i, j: (i, j)),
        compiler_params=_params("parallel", "parallel"),
        name="swiglu",
    )(x, wg, wu)


GATHER_UNROLL = 16


def _gather_body(idx_ref, src_ref, o_ref, sem, *, tm):
    base = pl.program_id(0) * tm

    def issue(c, carry):
        for u in range(GATHER_UNROLL):
            t = c * GATHER_UNROLL + u
            pltpu.make_async_copy(src_ref.at[pl.ds(idx_ref[base + t], 1)], o_ref.at[pl.ds(t, 1)], sem).start()
        return carry

    lax.fori_loop(0, tm // GATHER_UNROLL, issue, 0)
    pltpu.make_async_copy(src_ref.at[pl.ds(0, tm)], o_ref, sem).wait()


def _gather_rows(src, idx, *, tm):
    r = idx.shape[0]
    d = src.shape[1]
    assert src.dtype.itemsize == 4 and tm % GATHER_UNROLL == 0
    return pl.pallas_call(
        functools.partial(_gather_body, tm=tm),
        out_shape=jax.ShapeDtypeStruct((r, d), src.dtype),
        grid_spec=pltpu.PrefetchScalarGridSpec(
            num_scalar_prefetch=1,
            grid=(r // tm,),
            in_specs=[pl.BlockSpec(memory_space=pl.ANY)],
            out_specs=pl.BlockSpec((tm, d), lambda i, idx_ref: (i, 0)),
            scratch_shapes=[pltpu.SemaphoreType.DMA(())],
        ),
        compiler_params=_params("arbitrary"),
        name="gather_rows",
    )(idx, src)


def _routed_swiglu_body(te_ref, nu_ref, x_ref, wg_ref, wu_ref, o_ref):
    r = pl.program_id(1)

    @pl.when(r < nu_ref[0])
    def _():
        _swiglu_tile(x_ref, wg_ref, wu_ref, o_ref)

    @pl.when(r >= nu_ref[0])
    def _():
        o_ref[...] = jnp.zeros_like(o_ref)


def _routed_down_body(te_ref, nu_ref, x_ref, w_ref, o_ref):
    r = pl.program_id(1)

    @pl.when(r < nu_ref[0])
    def _():
        o_ref[...] = _pack_bf16_halves(jnp.dot(x_ref[...], w_ref[...].astype(BF16), preferred_element_type=F32))

    @pl.when(r >= nu_ref[0])
    def _():
        o_ref[...] = jnp.zeros_like(o_ref)


def _routed_call(body, x, ws, tile_expert, n_used, *, layer, n, tm, tn, out_dtype, name):
    rows, x_cols = x.shape
    kdim = ws[0].shape[-2]
    used_row = lambda r, nu: jnp.minimum(r, nu[0] - 1)
    w_spec = pl.BlockSpec((None, None, kdim, tn), lambda j, r, te, nu: (layer, te[r], 0, j))
    pack = 2 if out_dtype == jnp.uint32 else 1
    return pl.pallas_call(
        body,
        out_shape=jax.ShapeDtypeStruct((rows, n // pack), out_dtype),
        grid_spec=pltpu.PrefetchScalarGridSpec(
            num_scalar_prefetch=2,
            grid=(n // tn, rows // tm),
            in_specs=[pl.BlockSpec((tm, x_cols), lambda j, r, te, nu: (used_row(r, nu), 0))] + [w_spec] * len(ws),
            out_specs=pl.BlockSpec((tm, tn // pack), lambda j, r, te, nu: (r, j)),
        ),
        compiler_params=_params("parallel", "arbitrary"),
        name=name,
    )(tile_expert, n_used, x, *ws)


def _route_plan(route, n_experts, tm):
    nt = route.shape[0]
    picks = route[:, :TOP_K].astype(jnp.int32)
    flat = picks.T.reshape(-1)
    onehot = (flat[:, None] == jnp.arange(n_experts, dtype=jnp.int32)[None, :]).astype(jnp.int32)
    csum = jnp.cumsum(onehot, axis=0)
    rank = jnp.take_along_axis(csum, flat[:, None], axis=1)[:, 0] - 1
    counts = csum[-1]
    padded = (counts + tm - 1) // tm * tm
    ends = jnp.cumsum(padded)
    pos = (ends - padded)[flat] + rank
    n_rows = (TOP_K * nt // tm + n_experts) * tm
    tok = jnp.tile(jnp.arange(nt, dtype=jnp.int32), TOP_K)
    idx = (jnp.arange(n_rows, dtype=jnp.int32) % nt).at[pos].set(tok)
    n_used = (ends[-1] // tm).astype(jnp.int32)
    tile_start = jnp.minimum(jnp.arange(n_rows // tm, dtype=jnp.int32), n_used - 1) * tm
    tile_expert = jnp.sum((tile_start[:, None] >= ends[None, :]).astype(jnp.int32), axis=1)
    return idx, pos.reshape(TOP_K, nt), tile_expert, n_used.reshape(1)


def _qkv_post_body(qkv_ref, qn_ref, kn_ref, c_ref, s1_ref, s2_ref,
                   qa_ref, ka_ref, va_ref, qb_ref, kb_ref, vb_ref, st_ref, *, ha, ka, hb, kb, seg, qscale):
    c, s1, s2 = c_ref[...], s1_ref[...], s2_ref[...]
    qn, kn = qn_ref[...], kn_ref[...]
    col = 0
    st = 0

    def head(idx):
        return qkv_ref[:, idx * LANES:(idx + 1) * LANES]

    for h in range(ha):
        q = _rms_rows(head(col + h), qn)
        qa_ref[:, h * LANES:(h + 1) * LANES] = (_rope_lanes(q, c, s1, s2, seg) * qscale).astype(qa_ref.dtype)
    col += ha
    for h in range(ka):
        k = _rms_rows(head(col + h), kn)
        st_ref[:, (st + h) * LANES:(st + h + 1) * LANES] = k
        ka_ref[:, h * LANES:(h + 1) * LANES] = _rope_lanes(k, c, s1, s2, seg).astype(ka_ref.dtype)
    col += ka
    st += ka
    for h in range(ka):
        v = head(col + h)
        st_ref[:, (st + h) * LANES:(st + h + 1) * LANES] = v
        va_ref[:, h * LANES:(h + 1) * LANES] = v.astype(va_ref.dtype)
    col += ka
    st += ka
    for h in range(hb):
        qb_ref[:, h * LANES:(h + 1) * LANES] = (_rope_lanes(head(col + h), c, s1, s2, seg) * qscale).astype(qb_ref.dtype)
    col += hb
    for h in range(kb):
        k = head(col + h)
        st_ref[:, (st + h) * LANES:(st + h + 1) * LANES] = k
        kb_ref[:, h * LANES:(h + 1) * LANES] = _rope_lanes(k, c, s1, s2, seg).astype(kb_ref.dtype)
    col += kb
    st += kb
    for h in range(kb):
        v = head(col + h)
        st_ref[:, (st + h) * LANES:(st + h + 1) * LANES] = v
        vb_ref[:, h * LANES:(h + 1) * LANES] = v.astype(vb_ref.dtype)


def _qkv_post_call(qkv, qn_g, kn_g, layer, tabs, *, ha, ka, hb, kb, seg):
    nt, width = qkv.shape
    tm = _tile(nt, 256)
    row = lambda w: pl.BlockSpec((tm, w), lambda i: (i, 0))
    gain = pl.BlockSpec((None, 1, LANES), lambda i: (layer, 0, 0))
    widths = [ha * LANES, ka * LANES, ka * LANES, hb * LANES, kb * LANES, kb * LANES]
    st_w = 2 * (ka + kb) * LANES
    body = functools.partial(_qkv_post_body, ha=ha, ka=ka, hb=hb, kb=kb, seg=seg, qscale=LANES ** -0.5 * LOG2E)
    return pl.pallas_call(
        body,
        out_shape=[jax.ShapeDtypeStruct((nt, w), BF16) for w in widths] + [jax.ShapeDtypeStruct((nt, st_w), F32)],
        grid=(nt // tm,),
        in_specs=[row(width), gain, gain, row(LANES), row(LANES), row(LANES)],
        out_specs=[row(w) for w in widths] + [row(st_w)],
        compiler_params=_params("parallel"),
        name="qkv_post",
    )(qkv, qn_g.reshape(-1, 1, LANES), kn_g.reshape(-1, 1, LANES), *tabs)


def _attn_body(*refs, hps, k_lanes, v_lane, **kw):
    n_kparts = len(k_lanes)
    it = iter(refs)
    q_ref = next(it)
    k_refs = [next(it) for _ in range(n_kparts)]
    v_ref = next(it)
    kc_refs = [next(it) for _ in range(n_kparts)] if kw["has_ctx"] else None
    vc_ref = next(it) if kw["has_ctx"] else None
    sink_ref = next(it) if kw["has_sink"] else None
    o_ref = refs[-1]
    for hh in range(hps):
        lanes = lambda ref, off: ref.at[:, off:off + LANES]
        _attn_head(q_ref, [lanes(r, f(hh)) for r, f in zip(k_refs, k_lanes)], lanes(v_ref, v_lane(hh)),
                   [lanes(r, f(hh)) for r, f in zip(kc_refs, k_lanes)] if kc_refs else None,
                   lanes(vc_ref, v_lane(hh)) if vc_ref is not None else None,
                   sink_ref, o_ref, hh=hh, hps=hps, **kw)


def _attn_head(q_ref, k_refs, v_ref, kc_refs, vc_ref, sink_ref, o_ref, *, hh, hps, g, dq, dv, tq, tk, t,
               window, has_ctx, has_sink):
    h = pl.program_id(1) * hps + hh
    qi = pl.program_id(2)
    rows = g * tq
    q0 = hh * g * dq
    q = (jnp.concatenate([q_ref[:, q0 + i * dq:q0 + (i + 1) * dq] for i in range(g)], axis=0) if g > 1
         else q_ref[:, q0:q0 + dq])

    def scores(parts):
        k = parts[0] if len(parts) == 1 else jnp.concatenate(parts, axis=1)
        return lax.dot_general(q, k.astype(BF16), (((1,), (1,)), ((), ())), preferred_element_type=F32)

    def update(carry, s, v):
        m, acc = carry
        m_new = jnp.maximum(m, jnp.max(s, axis=1, keepdims=True))
        a = jnp.exp2(m - m_new)
        p = jnp.exp2(s - m_new)
        v1 = jnp.concatenate([v.astype(BF16), jnp.ones((v.shape[0], LANES), BF16)], axis=1)
        acc = a * acc + jnp.dot(p.astype(BF16), v1, preferred_element_type=F32)
        return m_new, acc

    carry = (jnp.full((rows, 1), NEG_INF, F32), jnp.zeros((rows, dv + LANES), F32))
    if window is None:
        for c in range(t // tk):
            s = scores([r[c * tk:(c + 1) * tk, :] for r in k_refs])
            carry = update(carry, s, v_ref[c * tk:(c + 1) * tk, :])
    else:
        first = (qi * tq - window) // tk
        for r_ in range((tq + 2 * window) // tk):
            c = first + r_
            start = pl.multiple_of(jnp.clip(c, 0, t // tk - 1) * tk, tk)
            s = scores([r[pl.ds(start, tk), :] for r in k_refs])
            qpos = qi * tq + lax.rem(lax.broadcasted_iota(jnp.int32, s.shape, 0), tq)
            kpos = c * tk + lax.broadcasted_iota(jnp.int32, s.shape, 1)
            ok = (jnp.abs(qpos - kpos) <= window) & (kpos >= 0) & (kpos < t)
            carry = update(carry, jnp.where(ok, s, NEG_INF), v_ref[pl.ds(start, tk), :])
    if has_ctx:
        carry = update(carry, scores([r[...] for r in kc_refs]), vc_ref[...])
    m, acc = carry
    l = acc[:, dv:dv + 1]
    acc = acc[:, :dv]
    if has_sink:
        sk = jnp.concatenate([jnp.full((tq, 1), sink_ref[0, h * g + i] * LOG2E, F32) for i in range(g)], axis=0)
        m_new = jnp.maximum(m, sk)
        a = jnp.exp2(m - m_new)
        l = a * l + jnp.exp2(sk - m_new)
        acc = a * acc
    out = acc / l
    o0 = hh * g * dv
    for i in range(g):
        o_ref[:, o0 + i * dv:o0 + (i + 1) * dv] = out[i * tq:(i + 1) * tq].astype(o_ref.dtype)


def _out_buffer(dead, shape):
    if dead.shape == shape and dead.dtype == BF16:
        return dead
    return jnp.zeros(shape, BF16)


def _attn_call(q, kparts, v, *, nb, t, row0, n_kv, g, dq, dv, tq, tk, hps, out_buf, out_col0,
               window=None, ctx=None, sink=None, name="attn"):
    assert row0 % t == 0 and t % tq == 0 and t % tk == 0 and out_col0 % (hps * g * dv) == 0 and n_kv % hps == 0
    qt = t // tq
    col_blk0 = out_col0 // (hps * g * dv)
    in_specs = [pl.BlockSpec((tq, hps * g * dq), lambda b, h, i: (row0 // tq + b * qt + i, h))]
    args = [q]

    def kv_spec(stride):
        if stride == 0:
            return pl.BlockSpec((t, LANES), lambda b, h, i: (row0 // t + b, 0))
        return pl.BlockSpec((t, hps * stride), lambda b, h, i: (row0 // t + b, h))

    lane_fn = lambda stride, off: (lambda hh: hh * stride + off)
    for arr, stride, off in kparts:
        in_specs.append(kv_spec(stride))
        args.append(arr)
    in_specs.append(kv_spec(v[1]))
    args.append(v[0])
    if ctx is not None:
        args += list(ctx[0])
        in_specs += list(ctx[1])
    if sink is not None:
        args.append(sink)
        in_specs.append(pl.BlockSpec(memory_space=pltpu.SMEM))
    args.append(out_buf)
    in_specs.append(pl.BlockSpec(memory_space=pl.ANY))
    body = functools.partial(_attn_body, hps=hps, k_lanes=[lane_fn(s_, o_) for _, s_, o_ in kparts],
                             v_lane=lane_fn(v[1], v[2]), g=g, dq=dq, dv=dv, tq=tq, tk=tk, t=t,
                             window=window, has_ctx=ctx is not None, has_sink=sink is not None)
    return pl.pallas_call(
        body,
        out_shape=jax.ShapeDtypeStruct(out_buf.shape, out_buf.dtype),
        grid=(nb, n_kv // hps, qt),
        in_specs=in_specs,
        out_specs=pl.BlockSpec((tq, hps * g * dv), lambda b, h, i: (row0 // tq + b * qt + i, col_blk0 + h)),
        input_output_aliases={len(args) - 1: 0},
        compiler_params=_params("parallel", "parallel", "arbitrary"),
        name=name,
    )(*args)


def _rope_tables(n_prompt, n_dec, t_dec, dim):
    seg = dim // 4
    d_axis = dim // 2
    row = jnp.repeat(jnp.arange(t_dec // GRID_W, dtype=F32), GRID_W)
    colp = (jnp.arange(t_dec) % GRID_W).astype(F32)
    inv = ROPE_THETA ** (-jnp.arange(0, d_axis, 2, dtype=F32) / d_axis)
    ar, ac = row[:, None] * inv, colp[:, None] * inv
    z = jnp.zeros_like(ar)
    pad = LANES - dim
    c = jnp.concatenate([jnp.cos(ar), jnp.cos(ar), jnp.cos(ac), jnp.cos(ac), jnp.ones((t_dec, pad), F32)], axis=1)
    s1 = jnp.concatenate([-jnp.sin(ar), z, -jnp.sin(ac), z, jnp.zeros((t_dec, pad), F32)], axis=1)
    s2 = jnp.concatenate([z, jnp.sin(ar), z, jnp.sin(ac), jnp.zeros((t_dec, pad), F32)], axis=1)

    def full(tab, fill):
        return jnp.concatenate([jnp.full((n_prompt, LANES), fill, F32)] + [tab] * n_dec, axis=0)

    return full(c, 1.0), full(s1, 0.0), full(s2, 0.0)


def kernel(x_prompt, x_sample, cache_k_a, cache_v_a, cache_k_b, cache_v_b, cache_ckv, cache_krope, c, c_ctx, w_mod, b_mod, ln1_g, ln1_b, ln2_g, ln2_b, w_qkv_ab, qn_a_g, kn_a_g, sink_b, w_o_ab, w_ff_gate, w_ff_up, w_ff_down, w_dq, qn_c_g, w_uq, w_dkv, kvn_c_g, w_ukv, w_o_c, w_router, we_gate, we_up, we_down):
    nb_p, t_p, d = x_prompt.shape
    nb_d, t_d, _ = x_sample.shape
    n_p, n_d = nb_p * t_p, nb_d * t_d
    nt = n_p + n_d
    depth = w_mod.shape[0]
    alpha = (2.0 * depth) ** 0.25
    past = cache_k_a.shape[2]
    ka, hd = cache_k_a.shape[3], cache_k_a.shape[4]
    kb = cache_k_b.shape[3]
    hb = sink_b.shape[1]
    ha = w_o_ab.shape[1] // hd - hb
    q_lora = w_dq.shape[-1]
    kv_lora = cache_ckv.shape[-1]
    rope_dim = cache_krope.shape[-1]
    n_e, ff_e = we_gate.shape[1], we_gate.shape[3]
    d_ff = w_ff_gate.shape[-1]
    mla_h = (w_uq.shape[-1] - w_ukv.shape[-1] + w_o_c.shape[1]) // rope_dim
    nope = w_uq.shape[-1] // mla_h - rope_dim
    v_dim = w_o_c.shape[1] // mla_h
    assert hd == LANES and nope == LANES and v_dim == LANES and rope_dim <= LANES
    assert nt % past == 0 and n_p % t_p == 0 and n_p % t_d == 0

    x = jnp.concatenate([x_prompt.reshape(n_p, d), x_sample.reshape(n_d, d)], axis=0)
    n_cond = 8
    cond = jnp.concatenate([c_ctx[None, :], c, jnp.zeros((n_cond - 1 - nb_d, d), F32)], axis=0)
    mod4 = _mod_call(cond, w_mod, b_mod).reshape(depth, n_cond, 6, d)

    tabs_ab = _rope_tables(n_p, nb_d, t_d, hd)
    tabs_c = _rope_tables(n_p, nb_d, t_d, rope_dim)

    tm = _tile(min(n_p, t_d), 1024)
    tmap = lambda layer: (lambda j, k: (layer, k, j))

    dq_c = 2 * LANES
    w_uq_p = jnp.pad(w_uq.reshape(w_uq.shape[0], q_lora, mla_h, nope + rope_dim),
                     ((0, 0), (0, 0), (0, 0), (0, dq_c - nope - rope_dim))).reshape(w_uq.shape[0], q_lora, mla_h * dq_c)
    dkv_w = kv_lora + LANES
    w_dkv_p = jnp.pad(w_dkv, ((0, 0), (0, 0), (0, dkv_w - w_dkv.shape[-1])))
    w_router_p = jnp.pad(w_router, ((0, 0), (0, 0), (0, LANES - n_e)))

    h = _modulate_call(x, mod4, 0, n_p, t_d)
    states_ab, states_c = [], []
    for layer in range(depth):
        j = layer // 2
        if layer % 2 == 0:
            n_qkv = w_qkv_ab.shape[-1]
            (qkv,) = _matmul(h, w_qkv_ab, w_map=tmap(j), n=n_qkv, tm=tm, tn=_tile(n_qkv, 512), name="qkv")
            qa, k_a, v_a, qb, k_b, v_b, st = _qkv_post_call(qkv, qn_a_g, kn_a_g, j, tabs_ab,
                                                            ha=ha, ka=ka, hb=hb, kb=kb, seg=hd // 4)
            states_ab.append(st[:n_p])
            ck = lambda arr: arr.reshape(nb_d, arr.shape[1], past, -1)
            ctx_spec = pl.BlockSpec((None, None, past, LANES), lambda b, hh, i: (b, j, 0, hh))
            common = dict(dq=hd, dv=hd)
            per_head = lambda arr: (arr, LANES, 0)
            o = _out_buffer(h, (nt, (ha + hb) * hd))
            o = _attn_call(qa, [per_head(k_a)], per_head(v_a), nb=nb_p, t=t_p, row0=0, n_kv=ka, g=ha // ka, hps=1,
                           out_buf=o, out_col0=0, tq=_tile(t_p, 256), tk=_tile(t_p, 512), name="attn_a_p", **common)
            o = _attn_call(qb, [per_head(k_b)], per_head(v_b), nb=nb_p, t=t_p, row0=0, n_kv=kb, g=hb // kb, hps=kb,
                           out_buf=o, out_col0=ha * hd, tq=_tile(t_p, 256), tk=_tile(t_p, 512),
                           sink=sink_b[j:j + 1], name="attn_b_p", **common)
            o = _attn_call(qa, [per_head(k_a)], per_head(v_a), nb=nb_d, t=t_d, row0=n_p, n_kv=ka, g=ha // ka, hps=1,
                           out_buf=o, out_col0=0, tq=_tile(t_d, 256), tk=_tile(t_d, 1024),
                           ctx=([ck(cache_k_a), ck(cache_v_a)], [ctx_spec, ctx_spec]), name="attn_a_d", **common)
            o = _attn_call(qb, [per_head(k_b)], per_head(v_b), nb=nb_d, t=t_d, row0=n_p, n_kv=kb, g=hb // kb, hps=1,
                           out_buf=o, out_col0=ha * hd, tq=_tile(t_d, 256), tk=WINDOW, window=WINDOW,
                           ctx=([ck(cache_k_b), ck(cache_v_b)], [ctx_spec, ctx_spec]),
                           sink=sink_b[j:j + 1], name="attn_b_d", **common)
            (sub,) = _matmul(o, w_o_ab, w_map=tmap(j), n=d, tm=tm, tn=_tile(d, 512), out=[(_tile(d, 512), BF16, d)],
                             name="wo_ab")
        else:
            (dqn,) = _matmul(h, w_dq, w_map=tmap(j), n=q_lora, tm=tm, tn=q_lora, tk=_tile(d, 2048),
                             out=[(q_lora, BF16, q_lora)], epi=_epi_rms,
                             extra=[qn_c_g.reshape(-1, 1, q_lora)],
                             extra_specs=[pl.BlockSpec((None, 1, q_lora), lambda i, jj, k: (j, 0, 0))], name="dq")
            tab_specs = [pl.BlockSpec((tm, LANES), lambda i, jj, k: (i, 0))] * 3
            nq = mla_h * dq_c
            (q_c,) = _matmul(dqn, w_uq_p, w_map=tmap(j), n=nq, tm=tm, tn=_tile(nq, 2048),
                             out=[(_tile(nq, 2048), BF16, nq)],
                             epi=functools.partial(_epi_mla_q, seg=rope_dim // 4, qscale=(nope + rope_dim) ** -0.5 * LOG2E),
                             extra=list(tabs_c), extra_specs=tab_specs, name="uq")
            st_c, ckv_b, kr_b = _matmul(
                h, w_dkv_p, w_map=tmap(j), n=dkv_w, tm=tm, tn=dkv_w, tk=_tile(d, 2048),
                out=[(dkv_w, F32, dkv_w), (kv_lora, BF16, kv_lora), (LANES, BF16, LANES)],
                epi=functools.partial(_epi_mla_kv, kv_lora=kv_lora, seg=rope_dim // 4),
                extra=[kvn_c_g.reshape(-1, 1, kv_lora)] + list(tabs_c),
                extra_specs=[pl.BlockSpec((None, 1, kv_lora), lambda i, jj, k: (j, 0, 0))] + tab_specs, name="dkv")
            states_c.append(st_c[:n_p])
            c_all = jnp.concatenate([ckv_b, cache_ckv[:, j].reshape(nb_d * past, kv_lora).astype(BF16)], axis=0)
            kr_all = jnp.concatenate(
                [kr_b, jnp.pad(cache_krope[:, j].reshape(nb_d * past, rope_dim), ((0, 0), (0, LANES - rope_dim))).astype(BF16)],
                axis=0)
            n_kv_cols = w_ukv.shape[-1]
            (kv,) = _matmul(c_all, w_ukv, w_map=tmap(j), n=n_kv_cols, tm=_tile(c_all.shape[0], 1024),
                            tn=_tile(n_kv_cols, 4096), out=[(_tile(n_kv_cols, 4096), BF16, n_kv_cols)], name="ukv")
            kv_w = 2 * LANES
            k_parts, v_part = [(kv, kv_w, 0), (kr_all, 0, 0)], (kv, kv_w, LANES)
            common = dict(dq=dq_c, dv=v_dim, n_kv=mla_h, g=1)
            o = _out_buffer(h, (nt, mla_h * v_dim))
            o = _attn_call(q_c, k_parts, v_part, nb=nb_p, t=t_p, row0=0, hps=_tile(mla_h, 8), out_buf=o, out_col0=0,
                           tq=_tile(t_p, 512), tk=_tile(t_p, 512), name="attn_c_p", **common)
            ctx_row = lambda b: nt // past + b
            ctx_specs = [pl.BlockSpec((past, kv_w), lambda b, hh, i: (ctx_row(b), hh)),
                         pl.BlockSpec((past, LANES), lambda b, hh, i: (ctx_row(b), 0)),
                         pl.BlockSpec((past, kv_w), lambda b, hh, i: (ctx_row(b), hh))]
            o = _attn_call(q_c, k_parts, v_part, nb=nb_d, t=t_d, row0=n_p, hps=1, out_buf=o, out_col0=0,
                           tq=_tile(t_d, 1024), tk=_tile(t_d, 1024),
                           ctx=([kv, kr_all, kv], ctx_specs), name="attn_c_d", **common)
            (sub,) = _matmul(o, w_o_c, w_map=tmap(j), n=d, tm=tm, tn=_tile(d, 512), out=[(_tile(d, 512), BF16, d)],
                             name="wo_c")

        ln_kw = dict(alpha=alpha, n_prompt=n_p, t_dec=t_d)
        if layer % 2 == 0:
            x, h = _ln_call(x, sub, mod4, ln1_g, ln1_b, layer=layer, gate_row=2, next_layer=layer, next_row=3, **ln_kw)
            act = _swiglu_call(h, w_ff_gate, w_ff_up, w_map=lambda jj: (j, 0, jj), n=d_ff, tm=tm, tn=_tile(d_ff, 256))
            (f,) = _matmul(act, w_ff_down, w_map=tmap(j), n=d, tm=tm, tn=_tile(d, 256), x_buffers=1,
                           out=[(_tile(d, 256), BF16, d)], name="ff_down")
        else:
            x, h, route = _ln_call(x, sub, mod4, ln1_g, ln1_b, layer=layer, gate_row=2, next_layer=layer, next_row=3,
                                   h_dtype=jnp.uint32, w_router=w_router_p[j], n_experts=n_e, **ln_kw)
            tm_e = _tile(TOP_K * nt, 512)
            idx, pos, tile_expert, n_used = _route_plan(route, n_e, tm_e)
            xs = _gather_rows(h, idx, tm=tm_e)
            act = _routed_call(_routed_swiglu_body, xs, [we_gate, we_up], tile_expert, n_used, layer=j,
                               n=ff_e, tm=tm_e, tn=_tile(ff_e, 512), out_dtype=BF16, name="moe_swiglu")
            ys = _routed_call(_routed_down_body, act, [we_down], tile_expert, n_used, layer=j,
                              n=d, tm=tm_e, tn=_tile(d, 1024), out_dtype=jnp.uint32, name="moe_down")
            f = _gather_rows(ys, pos.reshape(-1), tm=tm_e)
            ln_kw = dict(route=route, sub_group=_tile(d, 1024) // 2, **ln_kw)
        if layer + 1 < depth:
            x, h = _ln_call(x, f, mod4, ln2_g, ln2_b, layer=layer, gate_row=5, next_layer=layer + 1, next_row=0, **ln_kw)
        else:
            (y_prompt,) = _ln_call(x, f, mod4, ln2_g, ln2_b, layer=layer, gate_row=5, row0=0, n_rows=n_p, **ln_kw)
            (y_sample,) = _ln_call(x, f, mod4, ln2_g, ln2_b, layer=layer, gate_row=5, row0=n_p, n_rows=n_d, **ln_kw)

    y_prompt = y_prompt.reshape(nb_p, t_p, d)
    y_sample = y_sample.reshape(nb_d, t_d, d)
    st_ab = jnp.stack(states_ab, axis=0).reshape(len(states_ab), nb_p, t_p, -1)
    st_ab = jnp.transpose(st_ab, (1, 0, 2, 3))
    wa, wb = ka * hd, kb * hd
    new_k_a = st_ab[..., :wa].reshape(nb_p, -1, t_p, ka, hd)
    new_v_a = st_ab[..., wa:2 * wa].reshape(nb_p, -1, t_p, ka, hd)
    new_k_b = st_ab[..., 2 * wa:2 * wa + wb].reshape(nb_p, -1, t_p, kb, hd)
    new_v_b = st_ab[..., 2 * wa + wb:].reshape(nb_p, -1, t_p, kb, hd)
    st_c = jnp.transpose(jnp.stack(states_c, axis=0).reshape(len(states_c), nb_p, t_p, -1), (1, 0, 2, 3))
    new_ckv = st_c[..., :kv_lora]
    new_krope = st_c[..., kv_lora:kv_lora + rope_dim]
    return (y_prompt, y_sample, new_k_a, new_v_a, new_k_b, new_v_b, new_ckv, new_krope)
```
